```python
import jax, jax.numpy as jnp
from jax import lax
import numpy as np

D_MODEL = 1024
BATCH = 8
SEQ = 2048
DEPTH = 4

GRID_W = 64
CTX_LEN = 256

GLA_HEADS = 4
GLA_DK = 64
GLA_DV = 128
GLA_KW = GLA_HEADS * GLA_DK
GLA_VW = GLA_HEADS * GLA_DV
GATE_RANK = 16
GATE_TAU = 16.0
GLA_GATE_BIAS = 2.0
GLA_CHUNK = 64
RMS_EPS = 1e-6
CONV_W = 512
CONV_K = 3
ATT_HEADS = 8
ATT_KV_HEADS = 2
HEAD_DIM = 64
ATT_GROUP = ATT_HEADS // ATT_KV_HEADS
ATT_QW = ATT_HEADS * HEAD_DIM
ATT_KVW = ATT_KV_HEADS * HEAD_DIM
WINDOW = 128
ATT_BLOCK = 128
ROPE_THETA = 10000.0
NEG_INF = -1e30
D_FF = 2816
FFN_CONV_K = 3
MEM_WIDTHS = (GLA_KW, GLA_VW, GATE_RANK, GATE_RANK, ATT_KVW, ATT_KVW)
REST_WIDTHS = (GLA_KW, GLA_VW, CONV_W, CONV_W, CONV_W, ATT_QW, D_MODEL, D_MODEL, D_MODEL)
MEM_COLS = sum(MEM_WIDTHS)
IN_COLS = MEM_COLS + sum(REST_WIDTHS)
LN_EPS = 1e-5
DEEPNORM_ALPHA = (2 * DEPTH) ** 0.25
DEEPNORM_BETA = (8 * DEPTH) ** -0.25

kernel_name = 'hybrid_gla_conv_swa_deepnorm_dit'


def _split(p, widths):
    return jnp.split(p, np.cumsum(widths)[:-1].tolist(), axis=-1)


def heads(a, n):
    return a.reshape(a.shape[:-1] + (n, a.shape[-1] // n))


def layer_norm(x, w, b):
    xf = x.astype(jnp.float32)
    mu = jnp.mean(xf, axis=-1, keepdims=True)
    var = jnp.mean(jnp.square(xf - mu), axis=-1, keepdims=True)
    y = (xf - mu) * lax.rsqrt(var + LN_EPS)
    return (y * w.astype(jnp.float32) + b.astype(jnp.float32)).astype(x.dtype)


def dwconv3(a, w):
    k, ch = w.shape
    return lax.conv_general_dilated(a, w[:, None, :].astype(a.dtype), window_strides=(1,),
                                    padding=((k // 2, k // 2),),
                                    dimension_numbers=('NWC', 'WIO', 'NWC'),
                                    feature_group_count=ch)


def axial_rope_angles(rows):
    row_idx = jnp.repeat(jnp.arange(rows), GRID_W).astype(jnp.float32)
    col_idx = jnp.tile(jnp.arange(GRID_W), rows).astype(jnp.float32)
    half = HEAD_DIM // 2
    inv_freq = ROPE_THETA ** (-jnp.arange(0, half, 2, dtype=jnp.float32) / half)
    return jnp.concatenate([row_idx[:, None] * inv_freq, col_idx[:, None] * inv_freq], axis=-1)


def apply_axial_rope(a, ang):
    af = a.astype(jnp.float32)
    cos = jnp.cos(ang)[None, :, None, :]
    sin = jnp.sin(ang)[None, :, None, :]
    q = HEAD_DIM // 4
    segs = []
    for axis in range(2):
        seg = af[..., axis * 2 * q:(axis + 1) * 2 * q]
        x1, x2 = seg[..., :q], seg[..., q:]
        c_, s_ = cos[..., axis * q:(axis + 1) * q], sin[..., axis * q:(axis + 1) * q]
        segs += [x1 * c_ - x2 * s_, x1 * s_ + x2 * c_]
    return jnp.concatenate(segs, axis=-1).astype(a.dtype)


def gla_heads(a):
    return heads(a, GLA_HEADS).astype(jnp.float32)


def gla_query(a):
    return gla_heads(a) * (GLA_DK ** -0.5)


def gla_log_decay(g_lr, w_up, bias):
    z = (g_lr @ w_up + bias).astype(jnp.float32)
    return heads(jax.nn.log_sigmoid(z) / GATE_TAU, GLA_HEADS)


def _chunk(a):
    bsz, t, h, e = a.shape
    return a.reshape(bsz, t // GLA_CHUNK, GLA_CHUNK, h, e)


def gla_chunk_states(kc, vc, gc, s0):
    b = jnp.cumsum(gc, axis=2)
    b_last = b[:, :, -1]
    upd = jnp.einsum('bnlhk,bnlhv->bnhkv', kc * jnp.exp(b_last[:, :, None] - b), vc)

    def step(s, inp):
        decay, u = inp
        return decay[..., None] * s + u, s

    s_final, s_starts = lax.scan(step, s0, (jnp.moveaxis(jnp.exp(b_last), 1, 0),
                                           jnp.moveaxis(upd, 1, 0)))
    return b, jnp.moveaxis(s_starts, 0, 1), s_final


def gla_scan(q, k, v, g, s0):
    bsz, t = q.shape[0], q.shape[1]
    qc, kc, vc, gc = _chunk(q), _chunk(k), _chunk(v), _chunk(g)
    b, s_starts, s_final = gla_chunk_states(kc, vc, gc, s0)
    q_dec = qc * jnp.exp(b)
    inter = jnp.einsum('bnlhk,bnhkv->bnlhv', q_dec, s_starts)
    scores = jnp.einsum('bnlhk,bnmhk->bnhlm', q_dec, kc * jnp.exp(-b))
    scores = jnp.where(jnp.tril(jnp.ones((GLA_CHUNK, GLA_CHUNK), dtype=bool)), scores, 0.0)
    intra = jnp.einsum('bnhlm,bnmhv->bnlhv', scores, vc)
    return (inter + intra).reshape(bsz, t, GLA_HEADS, GLA_DV), s_final


def gla_final_state(k, v, g, s0):
    _, _, s_final = gla_chunk_states(_chunk(k), _chunk(v), _chunk(g), s0)
    return s_final


def gla_bidir(q, k, v, g_fwd, g_bwd, s0_fwd, s0_bwd):
    o_f, s_f = gla_scan(q, k, v, g_fwd, s0_fwd)
    fl = lambda a: jnp.flip(a, axis=1)
    o_b, s_b = gla_scan(fl(q), fl(k), fl(v), fl(g_bwd), s0_bwd)
    return o_f + fl(o_b), s_f, s_b


def gla_output(o, r, norm_w):
    o = o * lax.rsqrt(jnp.mean(jnp.square(o), axis=-1, keepdims=True) + RMS_EPS) * norm_w.astype(jnp.float32)
    o = o.reshape(o.shape[:2] + (GLA_VW,)).astype(r.dtype)
    return o * jax.nn.silu(r)


def short_conv(h, gate_b, gate_c, w):
    return gate_b * dwconv3(gate_c * h, w)


def window_attention(q, k, v, k_ctx, v_ctx, sink):
    bsz, t = q.shape[0], q.shape[1]
    nb = t // ATT_BLOCK
    qb = q.reshape(bsz, nb, ATT_BLOCK, ATT_KV_HEADS, ATT_GROUP, HEAD_DIM)

    def band(a):
        ap = jnp.pad(a, ((0, 0), (ATT_BLOCK, ATT_BLOCK), (0, 0), (0, 0)))
        ap = ap.reshape(bsz, nb + 2, ATT_BLOCK, ATT_KV_HEADS, HEAD_DIM)
        return jnp.concatenate([ap[:, :-2], ap[:, 1:-1], ap[:, 2:]], axis=2)

    kb, vb = band(k), band(v)
    scale = HEAD_DIM ** -0.5
    s_loc = jnp.einsum('bnqhgd,bnkhd->bnhgqk', qb, kb).astype(jnp.float32) * scale
    blk = jnp.arange(nb)[:, None] * ATT_BLOCK
    qpos = blk + jnp.arange(ATT_BLOCK)[None, :]
    kpos = blk - ATT_BLOCK + jnp.arange(3 * ATT_BLOCK)[None, :]
    valid = ((kpos[:, None, :] >= 0) & (kpos[:, None, :] < t)
             & (jnp.abs(qpos[:, :, None] - kpos[:, None, :]) <= WINDOW))
    s_loc = jnp.where(valid[None, :, None, None], s_loc, NEG_INF)
    s_ctx = jnp.einsum('bnqhgd,bchd->bnhgqc', qb, k_ctx).astype(jnp.float32) * scale
    s_sink = jnp.broadcast_to(sink.astype(jnp.float32).reshape(1, 1, ATT_KV_HEADS, ATT_GROUP, 1, 1),
                              s_loc.shape[:-1] + (1,))
    p = jax.nn.softmax(jnp.concatenate([s_loc, s_ctx, s_sink], axis=-1), axis=-1).astype(v.dtype)
    n_loc = 3 * ATT_BLOCK
    n_ctx = k_ctx.shape[1]
    o = (jnp.einsum('bnhgqk,bnkhd->bnqhgd', p[..., :n_loc], vb)
         + jnp.einsum('bnhgqc,bchd->bnqhgd', p[..., n_loc:n_loc + n_ctx], v_ctx))
    return o.reshape(bsz, t, ATT_QW)


def context_attention(q, k, v, sink):
    scale = HEAD_DIM ** -0.5
    s = jnp.einsum('bqhgd,bkhd->bhgqk', q, k).astype(jnp.float32) * scale
    s_sink = jnp.broadcast_to(sink.astype(jnp.float32).reshape(1, ATT_KV_HEADS, ATT_GROUP, 1, 1),
                              s.shape[:-1] + (1,))
    p = jax.nn.softmax(jnp.concatenate([s, s_sink], axis=-1), axis=-1)[..., :-1].astype(v.dtype)
    o = jnp.einsum('bhgqk,bkhd->bqhgd', p, v)
    return o.reshape(o.shape[:2] + (ATT_QW,))


def merge_branches(ya, yb, yc, ma, mb, mc, w_a, w_b, w_c, w_o):
    m = jax.nn.sigmoid(ma) * (ya @ w_a) + jax.nn.sigmoid(mb) * (yb @ w_b) + jax.nn.sigmoid(mc) * (yc @ w_c)
    return m @ w_o


def conv_ffn(h, w_up, w_conv, w_down):
    u = dwconv3(h @ w_up, w_conv)
    gate, val = jnp.split(u, 2, axis=-1)
    return (jax.nn.silu(gate) * val) @ w_down


def setup_inputs(seed: int = 0) -> dict:
    key = jax.random.key(seed)
    ks = jax.random.split(key, 32)
    nrm = lambda k, shape, s: jax.random.normal(k, shape, jnp.float32) * s
    L = DEPTH
    return {
        'x': nrm(ks[0], (BATCH, SEQ, D_MODEL), 1.0),
        'c': nrm(ks[1], (BATCH, D_MODEL), 1.0),
        'ctx': nrm(ks[2], (BATCH, CTX_LEN, D_MODEL), 1.0),
        'c_ctx': nrm(ks[3], (D_MODEL,), 1.0),
        'w_ada': nrm(ks[4], (L, D_MODEL, 6 * D_MODEL), 0.5 * D_MODEL ** -0.5),
        'b_ada': nrm(ks[5], (L, 6 * D_MODEL), 0.02),
        'w_in': nrm(ks[6], (L, D_MODEL, IN_COLS), D_MODEL ** -0.5),
        'gla_gate_up_f': nrm(ks[7], (L, GATE_RANK, GLA_KW), GATE_RANK ** -0.5),
        'gla_gate_bias_f': GLA_GATE_BIAS + nrm(ks[8], (L, GLA_KW), 0.1),
        'gla_gate_up_b': nrm(ks[9], (L, GATE_RANK, GLA_KW), GATE_RANK ** -0.5),
        'gla_gate_bias_b': GLA_GATE_BIAS + nrm(ks[10], (L, GLA_KW), 0.1),
        'gla_norm_w': 1.0 + nrm(ks[11], (L, GLA_DV), 0.02),
        'conv_w': nrm(ks[12], (L, CONV_K, CONV_W), CONV_K ** -0.5),
        'att_sink': nrm(ks[13], (L, ATT_HEADS), 0.5),
        'w_branch_a': nrm(ks[14], (L, GLA_VW, D_MODEL), GLA_VW ** -0.5),
        'w_branch_b': nrm(ks[15], (L, CONV_W, D_MODEL), CONV_W ** -0.5),
        'w_branch_c': nrm(ks[16], (L, ATT_QW, D_MODEL), ATT_QW ** -0.5),
        'w_out': nrm(ks[17], (L, D_MODEL, D_MODEL), DEEPNORM_BETA * D_MODEL ** -0.5),
        'ln1_w': 1.0 + nrm(ks[18], (L, D_MODEL), 0.02),
        'ln1_b': nrm(ks[19], (L, D_MODEL), 0.02),
        'ffn_up': nrm(ks[20], (L, D_MODEL, 2 * D_FF), D_MODEL ** -0.5),
        'ffn_conv': nrm(ks[21], (L, FFN_CONV_K, 2 * D_FF), FFN_CONV_K ** -0.5),
        'ffn_down': nrm(ks[22], (L, D_FF, D_MODEL), DEEPNORM_BETA * D_FF ** -0.5),
        'ln2_w': 1.0 + nrm(ks[23], (L, D_MODEL), 0.02),
        'ln2_b': nrm(ks[24], (L, D_MODEL), 0.02),
    }


def reference(x, c, ctx, c_ctx, w_ada, b_ada, w_in, gla_gate_up_f, gla_gate_bias_f,
              gla_gate_up_b, gla_gate_bias_b, gla_norm_w, conv_w, att_sink,
              w_branch_a, w_branch_b, w_branch_c, w_out, ln1_w, ln1_b,
              ffn_up, ffn_conv, ffn_down, ln2_w, ln2_b):
    bsz, seq_len = x.shape[0], x.shape[1]
    rows = seq_len // GRID_W
    ang = axial_rope_angles(rows)
    s_zero = jnp.zeros((bsz, GLA_HEADS, GLA_DK, GLA_DV), jnp.float32)
    fl = lambda a: jnp.flip(a, axis=1)
    xl, xc = x, ctx
    for layer in range(DEPTH):
        last = layer == DEPTH - 1
        mod = jax.nn.silu(c) @ w_ada[layer] + b_ada[layer]
        sh1, sc1, g1, sh2, sc2, g2 = jnp.split(mod[:, None, :], 6, axis=-1)
        n_mod_c = (2 if last else 6) * D_MODEL
        mod_c = jax.nn.silu(c_ctx) @ w_ada[layer][:, :n_mod_c] + b_ada[layer][:n_mod_c]
        mod_c = jnp.split(mod_c, n_mod_c // D_MODEL)
        hl = xl * (1 + sc1) + sh1
        hc = xc * (1 + mod_c[1]) + mod_c[0]

        pc = hc @ (w_in[layer][:, :MEM_COLS] if last else w_in[layer])
        parts_c = _split(pc, MEM_WIDTHS if last else MEM_WIDTHS + REST_WIDTHS)
        kg_c, vg_c, gfr_c, gbr_c, ka_c, va_c = parts_c[:6]
        k_c, v_c = gla_heads(kg_c), gla_heads(vg_c)
        gf_c = gla_log_decay(gfr_c, gla_gate_up_f[layer], gla_gate_bias_f[layer])
        gb_c = gla_log_decay(gbr_c, gla_gate_up_b[layer], gla_gate_bias_b[layer])
        k_att_c = heads(ka_c, ATT_KV_HEADS)
        v_att_c = heads(va_c, ATT_KV_HEADS)
        if last:
            sf_c = gla_final_state(k_c, v_c, gf_c, s_zero)
            sb_c = gla_final_state(fl(k_c), fl(v_c), fl(gb_c), s_zero)
        else:
            qg_c, rg_c, ch_c, cb_c, cc_c, qa_c, ma_c, mb_c, mc_c = parts_c[6:]
            o_c, sf_c, sb_c = gla_bidir(gla_query(qg_c), k_c, v_c, gf_c, gb_c, s_zero, s_zero)
            ya_c = gla_output(o_c, rg_c, gla_norm_w[layer])
            yb_c = short_conv(ch_c, cb_c, cc_c, conv_w[layer])
            q_att_c = heads(qa_c, ATT_HEADS).reshape(qa_c.shape[:2] + (ATT_KV_HEADS, ATT_GROUP, HEAD_DIM))
            yc_c = context_attention(q_att_c, k_att_c, v_att_c, att_sink[layer])
            mix_c = merge_branches(ya_c, yb_c, yc_c, ma_c, mb_c, mc_c, w_branch_a[layer],
                                   w_branch_b[layer], w_branch_c[layer], w_out[layer])
            xc_mid = layer_norm(DEEPNORM_ALPHA * xc + mod_c[2] * mix_c, ln1_w[layer], ln1_b[layer])

        pl = hl @ w_in[layer]
        (kg, vg, gfr, gbr, ka, va, qg, rg, ch, cb, cc, qa, ma, mb, mc) = _split(pl, MEM_WIDTHS + REST_WIDTHS)
        gf = gla_log_decay(gfr, gla_gate_up_f[layer], gla_gate_bias_f[layer])
        gb = gla_log_decay(gbr, gla_gate_up_b[layer], gla_gate_bias_b[layer])
        o_l, _, _ = gla_bidir(gla_query(qg), gla_heads(kg), gla_heads(vg), gf, gb, sf_c, sb_c)
        ya = gla_output(o_l, rg, gla_norm_w[layer])
        yb = short_conv(ch, cb, cc, conv_w[layer])
        q_att = apply_axial_rope(heads(qa, ATT_HEADS), ang).reshape(
            bsz, seq_len, ATT_KV_HEADS, ATT_GROUP, HEAD_DIM)
        k_att = apply_axial_rope(heads(ka, ATT_KV_HEADS), ang)
        yc = window_attention(q_att, k_att, heads(va, ATT_KV_HEADS), k_att_c, v_att_c, att_sink[layer])
        mix = merge_branches(ya, yb, yc, ma, mb, mc, w_branch_a[layer], w_branch_b[layer],
                             w_branch_c[layer], w_out[layer])
        xl = layer_norm(DEEPNORM_ALPHA * xl + g1 * mix, ln1_w[layer], ln1_b[layer])

        f_l = conv_ffn(xl * (1 + sc2) + sh2, ffn_up[layer], ffn_conv[layer], ffn_down[layer])
        xl = layer_norm(DEEPNORM_ALPHA * xl + g2 * f_l, ln2_w[layer], ln2_b[layer])
        if not last:
            f_c = conv_ffn(xc_mid * (1 + mod_c[4]) + mod_c[3], ffn_up[layer], ffn_conv[layer], ffn_down[layer])
            xc = layer_norm(DEEPNORM_ALPHA * xc_mid + mod_c[5] * f_c, ln2_w[layer], ln2_b[layer])
    return xl
```

```python
import functools

import numpy as np
import jax
import jax.numpy as jnp
from jax import lax
from jax.experimental import pallas as pl
from jax.experimental.pallas import tpu as pltpu

DEPTH_FOR_DEEPNORM = 4
DEEPNORM_ALPHA = (2 * DEPTH_FOR_DEEPNORM) ** 0.25
LN_EPS = 1e-5
RMS_EPS = 1e-6
GATE_TAU = 16.0
NEG_INF = -1e30
ROPE_THETA = 10000.0

GRID_W = 64
GLA_HEADS = 4
GLA_DK = 64
GLA_DV = 128
GLA_KW = GLA_HEADS * GLA_DK
GLA_VW = GLA_HEADS * GLA_DV
GATE_RANK = 16
CONV_W = 512
ATT_HEADS = 8
ATT_KV_HEADS = 2
ATT_GROUP = ATT_HEADS // ATT_KV_HEADS
HEAD_DIM = 64
ATT_QW = ATT_HEADS * HEAD_DIM
ATT_KVW = ATT_KV_HEADS * HEAD_DIM
WINDOW = 128
ATT_BLOCK = 128

LANES = 128
GLA_CHUNK = 128
HALO = 16
VMEM_LIMIT = 56 * 1024 * 1024

ACT = jnp.bfloat16
F32 = jnp.float32


def _cparams(sem):
    return pltpu.CompilerParams(dimension_semantics=sem, vmem_limit_bytes=VMEM_LIMIT)


def _split_hi_lo(a):
    hi = a.astype(ACT)
    lo = (a - hi.astype(F32)).astype(ACT)
    return hi, lo


def _dot(a, b):
    return jnp.dot(a, b, preferred_element_type=F32)


def _dot_nt(a, b):
    return lax.dot_general(a, b, (((1,), (1,)), ((), ())), preferred_element_type=F32)


def _dot3(a, b):
    ah, al = _split_hi_lo(a)
    bh, bl = _split_hi_lo(b)
    return _dot(ah, bh) + _dot(ah, bl) + _dot(al, bh)


def _sigmoid(x):
    return 1.0 / (1.0 + jnp.exp(-x))


def _silu(x):
    return x * _sigmoid(x)


def _log_sigmoid(x):
    return jnp.minimum(x, 0.0) - jnp.log(1.0 + jnp.exp(-jnp.abs(x)))


def _layer_norm(y, w, b):
    mu = jnp.mean(y, axis=-1, keepdims=True)
    yc = y - mu
    var = jnp.mean(yc * yc, axis=-1, keepdims=True)
    return yc * lax.rsqrt(var + LN_EPS) * w + b


def _row_mod(mod, idx, row0, nrows, n_ctx):
    rows = row0 + lax.broadcasted_iota(jnp.int32, (nrows, 1), 0)
    return jnp.where(rows < n_ctx, mod[idx:idx + 1, :], mod[6 + idx:7 + idx, :])


def _mod_kernel(c_ref, w_ref, b_ref, o_ref):
    s = _silu(c_ref[...])
    o_ref[0] = _dot3(s, w_ref[0]) + b_ref[0]


def _modulation(cvec, w_ada, b_ada):
    depth, d, n = w_ada.shape
    tn = n // 4
    rows = cvec.shape[0]
    return pl.pallas_call(
        _mod_kernel,
        grid=(depth, n // tn),
        in_specs=[pl.BlockSpec((rows, d), lambda l, j: (0, 0)),
                  pl.BlockSpec((1, d, tn), lambda l, j: (l, 0, j)),
                  pl.BlockSpec((1, 1, tn), lambda l, j: (l, 0, j))],
        out_specs=pl.BlockSpec((1, rows, tn), lambda l, j: (l, 0, j)),
        out_shape=jax.ShapeDtypeStruct((depth, rows, n), F32),
        compiler_params=_cparams(("arbitrary", "arbitrary")),
    )(cvec, w_ada, b_ada.reshape(depth, 1, n))


_W_QK, _W_V, _W_R, _W_QA, _W_KVA, _W_CONV, _W_GATES, _W_GLR = 512, 512, 512, 512, 256, 1536, 3072, 128
_OFF = np.cumsum([0, _W_QK, _W_V, _W_R, _W_QA, _W_KVA, _W_CONV, _W_GATES, _W_GLR]).tolist()
IN_COLS_PADDED = _OFF[-1]


def _inproj_kernel(n_ctx, bm, x_ref, mod_ref, w_ref, wg_ref, bg_ref,
                   qk_ref, v_ref, r_ref, qa_ref, kva_ref, conv_ref, gates_ref, g_ref):
    row0 = pl.program_id(1) * bm
    mod = mod_ref[0]
    sh = _row_mod(mod, 0, row0, bm, n_ctx)
    sc = _row_mod(mod, 1, row0, bm, n_ctx)
    h = (x_ref[0] * (1.0 + sc) + sh).astype(ACT)

    def proj(c0, width):
        return _dot(h, w_ref[:, c0:c0 + width])

    qk = proj(_OFF[0], _W_QK)
    lane = lax.broadcasted_iota(jnp.int32, (1, _W_QK), 1)
    qk_ref[0] = (qk * jnp.where(lane < GLA_KW, GLA_DK ** -0.5, 1.0)).astype(ACT)
    v_ref[0] = proj(_OFF[1], _W_V).astype(ACT)
    r_ref[0] = proj(_OFF[2], _W_R).astype(ACT)
    qa_ref[0] = (proj(_OFF[3], _W_QA) * (HEAD_DIM ** -0.5)).astype(ACT)
    kva_ref[0] = proj(_OFF[4], _W_KVA).astype(ACT)
    for i in range(_W_CONV // 512):
        conv_ref[0, :, i * 512:(i + 1) * 512] = proj(_OFF[5] + i * 512, 512).astype(ACT)
    for i in range(_W_GATES // 1024):
        gates_ref[0, :, i * 1024:(i + 1) * 1024] = proj(_OFF[6] + i * 1024, 1024).astype(ACT)
    glr = proj(_OFF[7], _W_GLR)
    z = _dot3(glr, wg_ref[...]) + bg_ref[...]
    g_ref[0] = _log_sigmoid(z) * (1.0 / GATE_TAU)


def _inproj(xs, modb, w, wg, bg, n_ctx, bm):
    bsz, t, d = xs.shape
    row = lambda width: pl.BlockSpec((1, bm, width), lambda b, r: (b, r, 0))
    out = lambda width, dt: jax.ShapeDtypeStruct((bsz, t, width), dt)
    return pl.pallas_call(
        functools.partial(_inproj_kernel, n_ctx, bm),
        grid=(bsz, t // bm),
        in_specs=[row(d),
                  pl.BlockSpec((1, 16, d), lambda b, r: (b, 0, 0)),
                  pl.BlockSpec(w.shape, lambda b, r: (0, 0)),
                  pl.BlockSpec(wg.shape, lambda b, r: (0, 0)),
                  pl.BlockSpec(bg.shape, lambda b, r: (0, 0))],
        out_specs=[row(_W_QK), row(_W_V), row(_W_R), row(_W_QA), row(_W_KVA), row(_W_CONV),
                   row(_W_GATES), row(2 * GLA_KW)],
        out_shape=[out(_W_QK, ACT), out(_W_V, ACT), out(_W_R, ACT), out(_W_QA, ACT), out(_W_KVA, ACT),
                   out(_W_CONV, ACT), out(_W_GATES, ACT), out(2 * GLA_KW, F32)],
        compiler_params=_cparams(("arbitrary", "arbitrary")),
    )(xs, modb, w, wg, bg)


def _gla_kernel(n_ctx, qk_ref, v_ref, r_ref, g_ref, nw_ref, o_ref, b_scr, u_scr, dec_scr):
    t = qk_ref.shape[1]
    L = GLA_CHUNK
    nch = t // L
    nch_ctx = n_ctx // L
    kw = GLA_KW

    ri = lax.broadcasted_iota(jnp.int32, (L, L), 0)
    ci = lax.broadcasted_iota(jnp.int32, (L, L), 1)
    tri_f = (ci <= ri).astype(ACT)
    tri_b = (ci >= ri).astype(ACT)
    lane_k = lax.broadcasted_iota(jnp.int32, (1, kw), 1)
    head_masks = [(lane_k >= h * GLA_DK) & (lane_k < (h + 1) * GLA_DK) for h in range(GLA_HEADS)]

    def rows(c):
        return pl.ds(pl.multiple_of(c * L, L), L)

    def pass1(c, carry):
        rs = rows(c)
        g = g_ref[0, rs, :]
        g_hi, g_lo = _split_hi_lo(g)
        b_f = _dot(tri_f, g_hi[:, :kw]) + _dot(tri_f, g_lo[:, :kw])
        b_b = _dot(tri_b, g_hi[:, kw:]) + _dot(tri_b, g_lo[:, kw:])
        b_scr[rs, :kw] = b_f
        b_scr[rs, kw:] = b_b
        tot_f = b_f[L - 1:L, :]
        tot_b = b_b[0:1, :]
        k = qk_ref[0, rs, kw:].astype(F32)
        ku_f = (k * jnp.exp(tot_f - b_f)).astype(ACT)
        ku_b = (k * jnp.exp(tot_b - b_b)).astype(ACT)
        ku = jnp.concatenate([ku_f, ku_b], axis=1)
        vt = v_ref[0, rs, :].astype(F32).T.astype(ACT)
        u_full = _dot(vt, ku)
        u_f = jnp.zeros((GLA_DV, kw), F32)
        u_b = jnp.zeros((GLA_DV, kw), F32)
        for h in range(GLA_HEADS):
            blk = u_full[h * GLA_DV:(h + 1) * GLA_DV, :]
            u_f = u_f + jnp.where(head_masks[h], blk[:, :kw], 0.0)
            u_b = u_b + jnp.where(head_masks[h], blk[:, kw:], 0.0)
        u_scr[c, 0] = u_f
        u_scr[c, 1] = u_b
        dec_scr[c, 0:1, :] = jnp.exp(tot_f)
        dec_scr[c, 1:2, :] = jnp.exp(tot_b)
        return carry

    lax.fori_loop(0, nch, pass1, 0)

    def scan_f(c, s):
        u = u_scr[c, 0]
        u_scr[c, 0] = s
        return s * dec_scr[c, 0:1, :] + u

    lax.fori_loop(0, nch, scan_f, jnp.zeros((GLA_DV, kw), F32))

    def scan_b(i, s):
        c = jnp.where(i < nch_ctx, nch_ctx - 1 - i, nch - 1 - (i - nch_ctx))
        u = u_scr[c, 1]
        u_scr[c, 1] = s
        return s * dec_scr[c, 1:2, :] + u

    lax.fori_loop(0, nch, scan_b, jnp.zeros((GLA_DV, kw), F32))

    ri_st = lax.broadcasted_iota(jnp.int32, (GLA_HEADS * L, L), 0) & (L - 1)
    ci_st = lax.broadcasted_iota(jnp.int32, (GLA_HEADS * L, L), 1)
    tril = ci_st <= ri_st
    triu = ci_st >= ri_st
    nw = nw_ref[...]

    def pass2(c, carry):
        rs = rows(c)
        q = qk_ref[0, rs, :kw].astype(F32)
        k = qk_ref[0, rs, kw:].astype(F32)
        v = v_ref[0, rs, :]
        b_f = b_scr[rs, :kw]
        b_b = b_scr[rs, kw:]
        e_f = jnp.exp(b_f)
        e_b = jnp.exp(b_b)
        qd_f = q * e_f
        qd_b = q * e_b
        ki_f = (k / e_f).astype(ACT)
        ki_b = (k / e_b).astype(ACT)
        qs_f = jnp.concatenate([jnp.where(m, qd_f, 0.0) for m in head_masks], axis=0).astype(ACT)
        qs_b = jnp.concatenate([jnp.where(m, qd_b, 0.0) for m in head_masks], axis=0).astype(ACT)
        p = (jnp.where(tril, _dot_nt(qs_f, ki_f), 0.0)
             + jnp.where(triu, _dot_nt(qs_b, ki_b), 0.0)).astype(ACT)
        s_f = u_scr[c, 0].astype(ACT)
        s_b = u_scr[c, 1].astype(ACT)
        inter = _dot_nt(qs_f, s_f) + _dot_nt(qs_b, s_b)
        outs = []
        for h in range(GLA_HEADS):
            o_h = inter[h * L:(h + 1) * L, :] + _dot(p[h * L:(h + 1) * L, :], v[:, h * GLA_DV:(h + 1) * GLA_DV])
            ms = jnp.mean(o_h * o_h, axis=-1, keepdims=True)
            outs.append(o_h * lax.rsqrt(ms + RMS_EPS) * nw)
        o = jnp.concatenate(outs, axis=1)
        o_ref[0, rs, :] = (o * _silu(r_ref[0, rs, :].astype(F32))).astype(o_ref.dtype)
        return carry

    lax.fori_loop(0, nch, pass2, 0)


def _gla(qk, v, r, g, nw, n_ctx):
    bsz, t, _ = qk.shape
    nch = t // GLA_CHUNK
    full = lambda width: pl.BlockSpec((1, t, width), lambda b: (b, 0, 0))
    return pl.pallas_call(
        functools.partial(_gla_kernel, n_ctx),
        grid=(bsz,),
        in_specs=[full(2 * GLA_KW), full(GLA_VW), full(GLA_VW), full(2 * GLA_KW),
                  pl.BlockSpec((1, GLA_DV), lambda b: (0, 0))],
        out_specs=full(GLA_VW),
        out_shape=jax.ShapeDtypeStruct((bsz, t, GLA_VW), ACT),
        scratch_shapes=[pltpu.VMEM((t, 2 * GLA_KW), F32),
                        pltpu.VMEM((nch, 2, GLA_DV, GLA_KW), F32),
                        pltpu.VMEM((nch, 8, GLA_KW), F32)],
        compiler_params=_cparams(("arbitrary",)),
    )(qk, v, r, g, nw)


def _conv_kernel(n_ctx, h_ref, gb_ref, gc_ref, w_ref, o_ref):
    t = h_ref.shape[1]
    a = gc_ref[0].astype(F32) * h_ref[0].astype(F32)
    rows = lax.broadcasted_iota(jnp.int32, (t, 1), 0)
    prev = jnp.where((rows == 0) | (rows == n_ctx), 0.0, pltpu.roll(a, 1, 0))
    nxt = jnp.where((rows == n_ctx - 1) | (rows == t - 1), 0.0, pltpu.roll(a, t - 1, 0))
    w = w_ref[...]
    y = w[0:1, :] * prev + w[1:2, :] * a + w[2:3, :] * nxt
    o_ref[0] = (gb_ref[0].astype(F32) * y).astype(o_ref.dtype)


def _short_conv(conv3, w, n_ctx):
    bsz, t, _ = conv3.shape
    spec = lambda i: pl.BlockSpec((1, t, CONV_W), lambda b: (b, 0, i))
    return pl.pallas_call(
        functools.partial(_conv_kernel, n_ctx),
        grid=(bsz,),
        in_specs=[spec(0), spec(1), spec(2), pl.BlockSpec(w.shape, lambda b: (0, 0))],
        out_specs=pl.BlockSpec((1, t, CONV_W), lambda b: (b, 0, 0)),
        out_shape=jax.ShapeDtypeStruct((bsz, t, CONV_W), ACT),
        compiler_params=_cparams(("arbitrary",)),
    )(conv3, conv3, conv3, w)


def _rope(x, cos, sin_signed):
    lane = lax.broadcasted_iota(jnp.int32, (1, LANES), 1)
    quarter = HEAD_DIM // 4
    swapped = jnp.where((lane % (2 * quarter)) < quarter,
                        pltpu.roll(x, LANES - quarter, 1), pltpu.roll(x, quarter, 1))
    return x * cos + swapped * sin_signed


def _attn_kernel(n_ctx, sink_ref, q_ref, kv_ref, cosq_ref, sinq_ref, cos_all_ref, sin_all_ref, o_ref, kr_scr):
    t = kv_ref.shape[1]
    j = pl.program_id(1)
    blk = ATT_BLOCK
    n_loc = 3 * blk
    n_lat = t - n_ctx

    @pl.when(j == 0)
    def _():
        kr_scr[...] = _rope(kv_ref[0, :, :LANES].astype(F32), cos_all_ref[...], sin_all_ref[...]).astype(ACT)

    lane = lax.broadcasted_iota(jnp.int32, (1, LANES), 1)
    first_half = lane < HEAD_DIM
    cos = cosq_ref[...]
    sin = sinq_ref[...]
    qs = []
    for gi in range(ATT_GROUP):
        qg = _rope(q_ref[0, :, gi * LANES:(gi + 1) * LANES].astype(F32), cos, sin)
        qs.append(jnp.where(first_half, qg, 0.0))
        qs.append(jnp.where(first_half, 0.0, qg))
    qst = jnp.concatenate(qs, axis=0).astype(ACT)

    jl = j - n_ctx // blk
    start = jnp.clip((jl - 1) * blk, 0, n_lat - n_loc)
    loc = pl.ds(pl.multiple_of(n_ctx + start, blk), n_loc)
    s_ctx = _dot_nt(qst, kr_scr[0:n_ctx, :])
    s_loc = _dot_nt(qst, kr_scr[loc, :])
    q_base = jnp.where(jl >= 0, jl * blk, -(t + WINDOW + blk))
    qpos = q_base + lax.broadcasted_iota(jnp.int32, (blk, n_loc), 0)
    kpos = start + lax.broadcasted_iota(jnp.int32, (blk, n_loc), 1)
    valid = jnp.abs(qpos - kpos) <= WINDOW
    v_ctx = kv_ref[0, 0:n_ctx, LANES:]
    v_loc = kv_ref[0, loc, LANES:]

    halves = []
    for i in range(2 * ATT_GROUP):
        gi, hf = i // 2, i % 2
        sink = sink_ref[hf * ATT_GROUP + gi]
        sc = s_ctx[i * blk:(i + 1) * blk, :]
        sl = jnp.where(valid, s_loc[i * blk:(i + 1) * blk, :], NEG_INF)
        m = jnp.maximum(jnp.maximum(jnp.max(sc, axis=-1, keepdims=True),
                                    jnp.max(sl, axis=-1, keepdims=True)), sink)
        pc = jnp.exp(sc - m)
        pl_ = jnp.exp(sl - m)
        den = (jnp.sum(pc, axis=-1, keepdims=True) + jnp.sum(pl_, axis=-1, keepdims=True)
               + jnp.exp(sink - m))
        o = _dot(pc.astype(ACT), v_ctx) + _dot(pl_.astype(ACT), v_loc)
        halves.append(o / den)
    for gi in range(ATT_GROUP):
        o_ref[0, :, gi * LANES:(gi + 1) * LANES] = jnp.where(
            first_half, halves[2 * gi], halves[2 * gi + 1]).astype(o_ref.dtype)


def _attention(qa, kva, sink, cos, sin_signed, n_ctx):
    bsz, t, _ = qa.shape
    blk = ATT_BLOCK
    return pl.pallas_call(
        functools.partial(_attn_kernel, n_ctx),
        grid=(bsz, t // blk),
        in_specs=[pl.BlockSpec(memory_space=pltpu.SMEM),
                  pl.BlockSpec((1, blk, ATT_QW), lambda b, j: (b, j, 0)),
                  pl.BlockSpec((1, t, 2 * ATT_KVW), lambda b, j: (b, 0, 0)),
                  pl.BlockSpec((blk, LANES), lambda b, j: (j, 0)),
                  pl.BlockSpec((blk, LANES), lambda b, j: (j, 0)),
                  pl.BlockSpec((t, LANES), lambda b, j: (0, 0)),
                  pl.BlockSpec((t, LANES), lambda b, j: (0, 0))],
        out_specs=pl.BlockSpec((1, blk, ATT_QW), lambda b, j: (b, j, 0)),
        out_shape=jax.ShapeDtypeStruct((bsz, t, ATT_QW), ACT),
        scratch_shapes=[pltpu.VMEM((t, LANES), ACT)],
        compiler_params=_cparams(("arbitrary", "arbitrary")),
    )(sink, qa, kva, cos, sin_signed, cos, sin_signed)


def _merge_kernel(n_ctx, bm, ya_ref, yb_ref, yc_ref, ma_ref, mb_ref, mc_ref, x_ref, mod_ref,
                  wa_ref, wb_ref, wc_ref, wo_ref, lnw_ref, lnb_ref, o_ref):
    row0 = pl.program_id(1) * bm
    m = (_sigmoid(ma_ref[0].astype(F32)) * _dot(ya_ref[0], wa_ref[...])
         + _sigmoid(mb_ref[0].astype(F32)) * _dot(yb_ref[0], wb_ref[...])
         + _sigmoid(mc_ref[0].astype(F32)) * _dot(yc_ref[0], wc_ref[...]))
    mix = _dot(m.astype(ACT), wo_ref[...])
    gate = _row_mod(mod_ref[0], 2, row0, bm, n_ctx)
    o_ref[0] = _layer_norm(DEEPNORM_ALPHA * x_ref[0] + gate * mix, lnw_ref[...], lnb_ref[...])


def _merge(ya, yb, yc, gates, xs, modb, wa, wb, wc, wo, lnw, lnb, n_ctx, bm):
    bsz, t, d = xs.shape
    row = lambda width, i=0: pl.BlockSpec((1, bm, width), lambda b, r: (b, r, i))
    const = lambda a: pl.BlockSpec(a.shape, lambda b, r: (0,) * a.ndim)
    return pl.pallas_call(
        functools.partial(_merge_kernel, n_ctx, bm),
        grid=(bsz, t // bm),
        in_specs=[row(GLA_VW), row(CONV_W), row(ATT_QW), row(d, 0), row(d, 1), row(d, 2), row(d),
                  pl.BlockSpec((1, 16, d), lambda b, r: (b, 0, 0)),
                  const(wa), const(wb), const(wc), const(wo), const(lnw), const(lnb)],
        out_specs=row(d),
        out_shape=jax.ShapeDtypeStruct((bsz, t, d), F32),
        compiler_params=_cparams(("arbitrary", "arbitrary")),
    )(ya, yb, yc, gates, gates, gates, xs, modb, wa, wb, wc, wo, lnw, lnb)


def _ffn_kernel(n_ctx, rb, x_ref, mod_ref, wg_ref, wv_ref, cg_ref, cv_ref, wd_ref, lnw_ref, lnb_ref,
                o_ref, h_scr):
    t = x_ref.shape[1]
    d = x_ref.shape[2]
    nc = pl.program_id(1)
    nblk = t // rb
    ext = rb + 2 * HALO

    @pl.when(nc == 0)
    def _():
        h_scr[0:HALO, :] = jnp.zeros((HALO, d), ACT)
        h_scr[HALO + t:HALO + t + HALO, :] = jnp.zeros((HALO, d), ACT)
        mod = mod_ref[0]

        def fill(i, carry):
            r0 = pl.multiple_of(i * rb, rb)
            sh = _row_mod(mod, 3, r0, rb, n_ctx)
            sc = _row_mod(mod, 4, r0, rb, n_ctx)
            h_scr[pl.ds(HALO + r0, rb), :] = (x_ref[0, pl.ds(r0, rb), :] * (1.0 + sc) + sh).astype(ACT)
            return carry

        lax.fori_loop(0, nblk, fill, 0)

    cg = cg_ref[...]
    cv = cv_ref[...]

    def conv3(u, w, keep_prev, keep_next):
        prev = jnp.where(keep_prev, pltpu.roll(u, 1, 0)[HALO:HALO + rb, :], 0.0)
        nxt = jnp.where(keep_next, pltpu.roll(u, ext - 1, 0)[HALO:HALO + rb, :], 0.0)
        return w[0:1, :] * prev + w[1:2, :] * u[HALO:HALO + rb, :] + w[2:3, :] * nxt

    def block(i, carry):
        r0 = pl.multiple_of(i * rb, rb)
        he = h_scr[pl.ds(r0, ext), :]
        rows = r0 + lax.broadcasted_iota(jnp.int32, (rb, 1), 0)
        keep_prev = rows != n_ctx
        keep_next = rows != n_ctx - 1
        gate = conv3(_dot(he, wg_ref[...]), cg, keep_prev, keep_next)
        val = conv3(_dot(he, wv_ref[...]), cv, keep_prev, keep_next)
        part = _dot((_silu(gate) * val).astype(ACT), wd_ref[...])
        rs = pl.ds(r0, rb)

        @pl.when(nc == 0)
        def _():
            o_ref[0, rs, :] = part

        @pl.when(nc != 0)
        def _():
            o_ref[0, rs, :] += part

        return carry

    lax.fori_loop(0, nblk, block, 0)

    @pl.when(nc == pl.num_programs(1) - 1)
    def _():
        mod = mod_ref[0]
        lnw = lnw_ref[...]
        lnb = lnb_ref[...]

        def fin(i, carry):
            r0 = pl.multiple_of(i * rb, rb)
            rs = pl.ds(r0, rb)
            gate = _row_mod(mod, 5, r0, rb, n_ctx)
            o_ref[0, rs, :] = _layer_norm(DEEPNORM_ALPHA * x_ref[0, rs, :] + gate * o_ref[0, rs, :], lnw, lnb)
            return carry

        lax.fori_loop(0, nblk, fin, 0)


def _ffn(xs, modb, w_up, w_conv, w_down, lnw, lnb, n_ctx, rb, tn):
    bsz, t, d = xs.shape
    d_ff = w_down.shape[0]
    nch = d_ff // tn
    return pl.pallas_call(
        functools.partial(_ffn_kernel, n_ctx, rb),
        grid=(bsz, nch),
        in_specs=[pl.BlockSpec((1, t, d), lambda b, n: (b, 0, 0)),
                  pl.BlockSpec((1, 16, d), lambda b, n: (b, 0, 0)),
                  pl.BlockSpec((d, tn), lambda b, n: (0, n)),
                  pl.BlockSpec((d, tn), lambda b, n: (0, nch + n)),
                  pl.BlockSpec((3, tn), lambda b, n: (0, n)),
                  pl.BlockSpec((3, tn), lambda b, n: (0, nch + n)),
                  pl.BlockSpec((tn, d), lambda b, n: (n, 0)),
                  pl.BlockSpec((1, d), lambda b, n: (0, 0)),
                  pl.BlockSpec((1, d), lambda b, n: (0, 0))],
        out_specs=pl.BlockSpec((1, t, d), lambda b, n: (b, 0, 0)),
        out_shape=jax.ShapeDtypeStruct((bsz, t, d), F32),
        scratch_shapes=[pltpu.VMEM((t + 2 * HALO, d), ACT)],
        compiler_params=_cparams(("arbitrary", "arbitrary")),
    )(xs, modb, w_up, w_up, w_conv, w_conv, w_down, lnw, lnb)


def _rope_tables(n_ctx, seq):
    rows = seq // GRID_W
    row_idx = np.repeat(np.arange(rows), GRID_W).astype(np.float32)
    col_idx = np.tile(np.arange(GRID_W), rows).astype(np.float32)
    half = HEAD_DIM // 2
    inv_freq = (ROPE_THETA ** (-np.arange(0, half, 2, dtype=np.float32) / half)).astype(np.float32)
    ang_r = jnp.asarray(row_idx[:, None] * inv_freq)
    ang_c = jnp.asarray(col_idx[:, None] * inv_freq)
    cos = jnp.concatenate([jnp.cos(ang_r)] * 2 + [jnp.cos(ang_c)] * 2, axis=-1)
    sin = jnp.concatenate([-jnp.sin(ang_r), jnp.sin(ang_r), -jnp.sin(ang_c), jnp.sin(ang_c)], axis=-1)
    cos = jnp.concatenate([jnp.ones((n_ctx, HEAD_DIM), F32), cos], axis=0)
    sin = jnp.concatenate([jnp.zeros((n_ctx, HEAD_DIM), F32), sin], axis=0)
    return jnp.tile(cos, (1, 2)), jnp.tile(sin, (1, 2))


def _head_perm():
    perm = []
    for gi in range(ATT_GROUP):
        for hf in range(ATT_KV_HEADS):
            head = hf * ATT_GROUP + gi
            perm.extend(range(head * HEAD_DIM, (head + 1) * HEAD_DIM))
    return np.asarray(perm)


def _largest_divisor(n, cap, multiple):
    best = multiple
    for cand in range(multiple, cap + 1, multiple):
        if n % cand == 0:
            best = cand
    return best


def kernel(x, c, ctx, c_ctx, w_ada, b_ada, w_in, gla_gate_up_f, gla_gate_bias_f, gla_gate_up_b,
           gla_gate_bias_b, gla_norm_w, conv_w, att_sink, w_branch_a, w_branch_b, w_branch_c, w_out,
           ln1_w, ln1_b, ffn_up, ffn_conv, ffn_down, ln2_w, ln2_b):
    bsz, seq, d = x.shape
    n_ctx = ctx.shape[1]
    depth = w_in.shape[0]
    t = n_ctx + seq
    assert n_ctx % GLA_CHUNK == 0 and seq % GLA_CHUNK == 0 and seq >= 3 * ATT_BLOCK
    bm = _largest_divisor(t, 384, 128)
    rb = _largest_divisor(t, 768, 128)
    d_ff = ffn_down.shape[1]
    tn = 256

    xs = jnp.concatenate([ctx, x], axis=1)

    cvec = jnp.zeros((16, d), F32).at[:bsz].set(c).at[bsz].set(c_ctx)
    mod = _modulation(cvec, w_ada, b_ada)
    mod_lat = mod[:, :bsz].reshape(depth, bsz, 6, d)
    mod_ctx = jnp.broadcast_to(mod[:, bsz].reshape(depth, 1, 6, d), (depth, bsz, 6, d))
    modb = jnp.concatenate([mod_ctx, mod_lat, jnp.zeros((depth, bsz, 4, d), F32)], axis=2)

    perm = _head_perm()
    o_mem = np.cumsum([0, GLA_KW, GLA_VW, GATE_RANK, GATE_RANK, ATT_KVW, ATT_KVW]).tolist()
    o_rest = (o_mem[-1] + np.cumsum([0, GLA_KW, GLA_VW, CONV_W, CONV_W, CONV_W, ATT_QW, d, d, d])).tolist()
    seg = lambda lo, hi: w_in[:, :, lo:hi]
    qa_w = seg(o_rest[5], o_rest[6])[:, :, perm]
    w_in_r = jnp.concatenate(
        [seg(o_rest[0], o_rest[1]), seg(o_mem[0], o_mem[1]),
         seg(o_mem[1], o_mem[2]),
         seg(o_rest[1], o_rest[2]),
         qa_w,
         seg(o_mem[4], o_mem[6]),
         seg(o_rest[2], o_rest[5]),
         seg(o_rest[6], o_rest[9]),
         seg(o_mem[2], o_mem[4]),
         jnp.zeros((depth, d, _W_GLR - 2 * GATE_RANK), F32)], axis=2).astype(ACT)
    zpad = jnp.zeros((depth, GATE_RANK, GLA_KW), F32)
    w_gate = jnp.concatenate(
        [jnp.concatenate([gla_gate_up_f, zpad], axis=2), jnp.concatenate([zpad, gla_gate_up_b], axis=2),
         jnp.zeros((depth, _W_GLR - 2 * GATE_RANK, 2 * GLA_KW), F32)], axis=1)
    b_gate = jnp.concatenate([gla_gate_bias_f, gla_gate_bias_b], axis=1).reshape(depth, 1, 2 * GLA_KW)

    cos, sin_signed = _rope_tables(n_ctx, seq)
    nw = gla_norm_w.reshape(depth, 1, GLA_DV)
    sink_p = att_sink
    wa = w_branch_a.astype(ACT)
    wb = w_branch_b.astype(ACT)
    wc = w_branch_c[:, perm, :].astype(ACT)
    wo = w_out.astype(ACT)
    w_up = ffn_up.astype(ACT)
    w_dn = ffn_down.astype(ACT)

    for l in range(depth):
        qk, v, r, qa, kva, conv3, gates, g = _inproj(xs, modb[l], w_in_r[l], w_gate[l], b_gate[l], n_ctx, bm)
        ya = _gla(qk, v, r, g, nw[l], n_ctx)
        yb = _short_conv(conv3, conv_w[l], n_ctx)
        yc = _attention(qa, kva, sink_p[l], cos, sin_signed, n_ctx)
        xs = _merge(ya, yb, yc, gates, xs, modb[l], wa[l], wb[l], wc[l], wo[l],
                    ln1_w[l].reshape(1, d), ln1_b[l].reshape(1, d), n_ctx, bm)
        xs = _ffn(xs, modb[l], w_up[l], ffn_conv[l], w_dn[l],
                  ln2_w[l].reshape(1, d), ln2_b[l].reshape(1, d), n_ctx, rb, tn)
    return xs[:, n_ctx:, :]
```

```python
import functools

import numpy as np
import jax
import jax.numpy as jnp
from jax import lax
from jax.experimental import pallas as pl
from jax.experimental.pallas import tpu as pltpu

DEPTH_FOR_DEEPNORM = 4
DEEPNORM_ALPHA = (2 * DEPTH_FOR_DEEPNORM) ** 0.25
LN_EPS = 1e-5
RMS_EPS = 1e-6
GATE_TAU = 16.0
NEG_INF = -1e30
ROPE_THETA = 10000.0

GRID_W = 64
GLA_HEADS = 4
GLA_DK = 64
GLA_DV = 128
GLA_KW = GLA_HEADS * GLA_DK
GLA_VW = GLA_HEADS * GLA_DV
GATE_RANK = 16
CONV_W = 512
ATT_HEADS = 8
ATT_KV_HEADS = 2
ATT_GROUP = ATT_HEADS // ATT_KV_HEADS
HEAD_DIM = 64
ATT_QW = ATT_HEADS * HEAD_DIM
ATT_KVW = ATT_KV_HEADS * HEAD_DIM
WINDOW = 128
ATT_BLOCK = 128

LANES = 128
GLA_CHUNK = 128
HALO = 16
VMEM_LIMIT = 56 * 1024 * 1024

ACT = jnp.bfloat16
F32 = jnp.float32


def _cparams(sem):
    return pltpu.CompilerParams(dimension_semantics=sem, vmem_limit_bytes=VMEM_LIMIT)


def _split_hi_lo(a):
    hi = a.astype(ACT)
    lo = (a - hi.astype(F32)).astype(ACT)
    return hi, lo


def _dot(a, b):
    return jnp.dot(a, b, preferred_element_type=F32)


def _dot_nt(a, b):
    return lax.dot_general(a, b, (((1,), (1,)), ((), ())), preferred_element_type=F32)


def _dot3(a, b):
    ah, al = _split_hi_lo(a)
    bh, bl = _split_hi_lo(b)
    return _dot(ah, bh) + _dot(ah, bl) + _dot(al, bh)


def _sigmoid(x):
    return 1.0 / (1.0 + jnp.exp(-x))


def _silu(x):
    return x * _sigmoid(x)


def _log_sigmoid(x):
    return jnp.minimum(x, 0.0) - jnp.log(1.0 + jnp.exp(-jnp.abs(x)))


def _layer_norm(y, w, b):
    mu = jnp.mean(y, axis=-1, keepdims=True)
    yc = y - mu
    var = jnp.mean(yc * yc, axis=-1, keepdims=True)
    return yc * lax.rsqrt(var + LN_EPS) * w + b


def _row_mod(mod, idx, row0, nrows, n_ctx):
    rows = row0 + lax.broadcasted_iota(jnp.int32, (nrows, 1), 0)
    return jnp.where(rows < n_ctx, mod[idx:idx + 1, :], mod[6 + idx:7 + idx, :])


def _mod_kernel(c_ref, w_ref, b_ref, o_ref):
    s = _silu(c_ref[...])
    o_ref[0] = _dot3(s, w_ref[0]) + b_ref[0]


def _modulation(cvec, w_ada, b_ada):
    depth, d, n = w_ada.shape
    tn = n // 4
    rows = cvec.shape[0]
    return pl.pallas_call(
        _mod_kernel,
        grid=(depth, n // tn),
        in_specs=[pl.BlockSpec((rows, d), lambda l, j: (0, 0)),
                  pl.BlockSpec((1, d, tn), lambda l, j: (l, 0, j)),
                  pl.BlockSpec((1, 1, tn), lambda l, j: (l, 0, j))],
        out_specs=pl.BlockSpec((1, rows, tn), lambda l, j: (l, 0, j)),
        out_shape=jax.ShapeDtypeStruct((depth, rows, n), F32),
        compiler_params=_cparams(("arbitrary", "arbitrary")),
    )(cvec, w_ada, b_ada.reshape(depth, 1, n))


_W_QK, _W_V, _W_R, _W_QA, _W_KVA, _W_CONV, _W_GATES, _W_GLR = 512, 512, 512, 512, 256, 1536, 3072, 128
_OFF = np.cumsum([0, _W_QK, _W_V, _W_R, _W_QA, _W_KVA, _W_CONV, _W_GATES, _W_GLR]).tolist()
IN_COLS_PADDED = _OFF[-1]


def _inproj_kernel(n_ctx, bm, x_ref, mod_ref, w_ref, wg_ref, bg_ref,
                   qk_ref, v_ref, r_ref, qa_ref, kva_ref, conv_ref, gates_ref, g_ref):
    row0 = pl.program_id(1) * bm
    mod = mod_ref[0]
    sh = _row_mod(mod, 0, row0, bm, n_ctx)
    sc = _row_mod(mod, 1, row0, bm, n_ctx)
    h = (x_ref[0] * (1.0 + sc) + sh).astype(ACT)

    def proj(c0, width):
        return _dot(h, w_ref[:, c0:c0 + width])

    qk = proj(_OFF[0], _W_QK)
    lane = lax.broadcasted_iota(jnp.int32, (1, _W_QK), 1)
    qk_ref[0] = (qk * jnp.where(lane < GLA_KW, GLA_DK ** -0.5, 1.0)).astype(ACT)
    v_ref[0] = proj(_OFF[1], _W_V).astype(ACT)
    r_ref[0] = proj(_OFF[2], _W_R).astype(ACT)
    qa_ref[0] = (proj(_OFF[3], _W_QA) * (HEAD_DIM ** -0.5)).astype(ACT)
    kva_ref[0] = proj(_OFF[4], _W_KVA).astype(ACT)
    for i in range(_W_CONV // 512):
        conv_ref[0, :, i * 512:(i + 1) * 512] = proj(_OFF[5] + i * 512, 512).astype(ACT)
    for i in range(_W_GATES // 1024):
        gates_ref[0, :, i * 1024:(i + 1) * 1024] = proj(_OFF[6] + i * 1024, 1024).astype(ACT)
    glr = proj(_OFF[7], _W_GLR)
    z = _dot3(glr, wg_ref[...]) + bg_ref[...]
    g_ref[0] = _log_sigmoid(z) * (1.0 / GATE_TAU)


def _inproj(xs, modb, w, wg, bg, n_ctx, bm):
    bsz, t, d = xs.shape
    row = lambda width: pl.BlockSpec((1, bm, width), lambda b, r: (b, r, 0))
    out = lambda width, dt: jax.ShapeDtypeStruct((bsz, t, width), dt)
    return pl.pallas_call(
        functools.partial(_inproj_kernel, n_ctx, bm),
        grid=(bsz, t // bm),
        in_specs=[row(d),
                  pl.BlockSpec((1, 16, d), lambda b, r: (b, 0, 0)),
                  pl.BlockSpec(w.shape, lambda b, r: (0, 0)),
                  pl.BlockSpec(wg.shape, lambda b, r: (0, 0)),
                  pl.BlockSpec(bg.shape, lambda b, r: (0, 0))],
        out_specs=[row(_W_QK), row(_W_V), row(_W_R), row(_W_QA), row(_W_KVA), row(_W_CONV),
                   row(_W_GATES), row(2 * GLA_KW)],
        out_shape=[out(_W_QK, ACT), out(_W_V, ACT), out(_W_R, ACT), out(_W_QA, ACT), out(_W_KVA, ACT),
                   out(_W_CONV, ACT), out(_W_GATES, ACT), out(2 * GLA_KW, F32)],
        compiler_params=_cparams(("arbitrary", "arbitrary")),
    )(xs, modb, w, wg, bg)


def _gla_kernel(n_ctx, qk_ref, v_ref, r_ref, g_ref, nw_ref, o_ref, b_scr, u_scr, dec_scr):
    t = qk_ref.shape[1]
    L = GLA_CHUNK
    nch = t // L
    nch_ctx = n_ctx // L
    kw = GLA_KW

    ri = lax.broadcasted_iota(jnp.int32, (L, L), 0)
    ci = lax.broadcasted_iota(jnp.int32, (L, L), 1)
    tri_f = (ci <= ri).astype(ACT)
    tri_b = (ci >= ri).astype(ACT)
    lane_k = lax.broadcasted_iota(jnp.int32, (1, kw), 1)
    head_masks = [(lane_k >= h * GLA_DK) & (lane_k < (h + 1) * GLA_DK) for h in range(GLA_HEADS)]

    def rows(c):
        return pl.ds(pl.multiple_of(c * L, L), L)

    def pass1(c, carry):
        rs = rows(c)
        g = g_ref[0, rs, :]
        g_hi, g_lo = _split_hi_lo(g)
        b_f = _dot(tri_f, g_hi[:, :kw]) + _dot(tri_f, g_lo[:, :kw])
        b_b = _dot(tri_b, g_hi[:, kw:]) + _dot(tri_b, g_lo[:, kw:])
        b_scr[rs, :kw] = b_f
        b_scr[rs, kw:] = b_b
        tot_f = b_f[L - 1:L, :]
        tot_b = b_b[0:1, :]
        k = qk_ref[0, rs, kw:].astype(F32)
        ku_f = (k * jnp.exp(tot_f - b_f)).astype(ACT)
        ku_b = (k * jnp.exp(tot_b - b_b)).astype(ACT)
        ku = jnp.concatenate([ku_f, ku_b], axis=1)
        vt = v_ref[0, rs, :].astype(F32).T.astype(ACT)
        u_full = _dot(vt, ku)
        u_f = jnp.zeros((GLA_DV, kw), F32)
        u_b = jnp.zeros((GLA_DV, kw), F32)
        for h in range(GLA_HEADS):
            blk = u_full[h * GLA_DV:(h + 1) * GLA_DV, :]
            u_f = u_f + jnp.where(head_masks[h], blk[:, :kw], 0.0)
            u_b = u_b + jnp.where(head_masks[h], blk[:, kw:], 0.0)
        u_scr[c, 0] = u_f
        u_scr[c, 1] = u_b
        dec_scr[c, 0:1, :] = jnp.exp(tot_f)
        dec_scr[c, 1:2, :] = jnp.exp(tot_b)
        return carry

    lax.fori_loop(0, nch, pass1, 0)

    def scan_f(c, s):
        u = u_scr[c, 0]
        u_scr[c, 0] = s
        return s * dec_scr[c, 0:1, :] + u

    lax.fori_loop(0, nch, scan_f, jnp.zeros((GLA_DV, kw), F32))

    def scan_b(i, s):
        c = jnp.where(i < nch_ctx, nch_ctx - 1 - i, nch - 1 - (i - nch_ctx))
        u = u_scr[c, 1]
        u_scr[c, 1] = s
        return s * dec_scr[c, 1:2, :] + u

    lax.fori_loop(0, nch, scan_b, jnp.zeros((GLA_DV, kw), F32))

    ri_st = lax.broadcasted_iota(jnp.int32, (GLA_HEADS * L, L), 0) & (L - 1)
    ci_st = lax.broadcasted_iota(jnp.int32, (GLA_HEADS * L, L), 1)
    tril = ci_st <= ri_st
    triu = ci_st >= ri_st
    nw = nw_ref[...]

    def pass2(c, carry):
        rs = rows(c)
        q = qk_ref[0, rs, :kw].astype(F32)
        k = qk_ref[0, rs, kw:].astype(F32)
        v = v_ref[0, rs, :]
        b_f = b_scr[rs, :kw]
        b_b = b_scr[rs, kw:]
        e_f = jnp.exp(b_f)
        e_b = jnp.exp(b_b)
        qd_f = q * e_f
        qd_b = q * e_b
        ki_f = (k / e_f).astype(ACT)
        ki_b = (k / e_b).astype(ACT)
        qs_f = jnp.concatenate([jnp.where(m, qd_f, 0.0) for m in head_masks], axis=0).astype(ACT)
        qs_b = jnp.concatenate([jnp.where(m, qd_b, 0.0) for m in head_masks], axis=0).astype(ACT)
        p = (jnp.where(tril, _dot_nt(qs_f, ki_f), 0.0)
             + jnp.where(triu, _dot_nt(qs_b, ki_b), 0.0)).astype(ACT)
        s_f = u_scr[c, 0].astype(ACT)
        s_b = u_scr[c, 1].astype(ACT)
        inter = _dot_nt(qs_f, s_f) + _dot_nt(qs_b, s_b)
        outs = []
        for h in range(GLA_HEADS):
            o_h = inter[h * L:(h + 1) * L, :] + _dot(p[h * L:(h + 1) * L, :], v[:, h * GLA_DV:(h + 1) * GLA_DV])
            ms = jnp.mean(o_h * o_h, axis=-1, keepdims=True)
            outs.append(o_h * lax.rsqrt(ms + RMS_EPS) * nw)
        o = jnp.concatenate(outs, axis=1)
        o_ref[0, rs, :] = (o * _silu(r_ref[0, rs, :].astype(F32))).astype(o_ref.dtype)
        return carry

    lax.fori_loop(0, nch, pass2, 0)


def _gla(qk, v, r, g, nw, n_ctx):
    bsz, t, _ = qk.shape
    nch = t // GLA_CHUNK
    full = lambda width: pl.BlockSpec((1, t, width), lambda b: (b, 0, 0))
    return pl.pallas_call(
        functools.partial(_gla_kernel, n_ctx),
        grid=(bsz,),
        in_specs=[full(2 * GLA_KW), full(GLA_VW), full(GLA_VW), full(2 * GLA_KW),
                  pl.BlockSpec((1, GLA_DV), lambda b: (0, 0))],
        out_specs=full(GLA_VW),
        out_shape=jax.ShapeDtypeStruct((bsz, t, GLA_VW), ACT),
        scratch_shapes=[pltpu.VMEM((t, 2 * GLA_KW), F32),
                        pltpu.VMEM((nch, 2, GLA_DV, GLA_KW), F32),
                        pltpu.VMEM((nch, 8, GLA_KW), F32)],
        compiler_params=_cparams(("arbitrary",)),
    )(qk, v, r, g, nw)


def _conv_kernel(n_ctx, h_ref, gb_ref, gc_ref, w_ref, o_ref):
    t = h_ref.shape[1]
    a = gc_ref[0].astype(F32) * h_ref[0].astype(F32)
    rows = lax.broadcasted_iota(jnp.int32, (t, 1), 0)
    prev = jnp.where((rows == 0) | (rows == n_ctx), 0.0, pltpu.roll(a, 1, 0))
    nxt = jnp.where((rows == n_ctx - 1) | (rows == t - 1), 0.0, pltpu.roll(a, t - 1, 0))
    w = w_ref[...]
    y = w[0:1, :] * prev + w[1:2, :] * a + w[2:3, :] * nxt
    o_ref[0] = (gb_ref[0].astype(F32) * y).astype(o_ref.dtype)


def _short_conv(conv3, w, n_ctx):
    bsz, t, _ = conv3.shape
    spec = lambda i: pl.BlockSpec((1, t, CONV_W), lambda b: (b, 0, i))
    return pl.pallas_call(
        functools.partial(_conv_kernel, n_ctx),
        grid=(bsz,),
        in_specs=[spec(0), spec(1), spec(2), pl.BlockSpec(w.shape, lambda b: (0, 0))],
        out_specs=pl.BlockSpec((1, t, CONV_W), lambda b: (b, 0, 0)),
        out_shape=jax.ShapeDtypeStruct((bsz, t, CONV_W), ACT),
        compiler_params=_cparams(("arbitrary",)),
    )(conv3, conv3, conv3, w)


def _rope(x, cos, sin_signed):
    lane = lax.broadcasted_iota(jnp.int32, (1, LANES), 1)
    quarter = HEAD_DIM // 4
    swapped = jnp.where((lane % (2 * quarter)) < quarter,
                        pltpu.roll(x, LANES - quarter, 1), pltpu.roll(x, quarter, 1))
    return x * cos + swapped * sin_signed


def _attn_kernel(n_ctx, sink_ref, q_ref, kv_ref, cosq_ref, sinq_ref, cos_all_ref, sin_all_ref, o_ref, kr_scr):
    t = kv_ref.shape[1]
    j = pl.program_id(1)
    blk = ATT_BLOCK
    n_loc = 3 * blk
    n_lat = t - n_ctx

    @pl.when(j == 0)
    def _():
        kr_scr[...] = _rope(kv_ref[0, :, :LANES].astype(F32), cos_all_ref[...], sin_all_ref[...]).astype(ACT)

    lane = lax.broadcasted_iota(jnp.int32, (1, LANES), 1)
    first_half = lane < HEAD_DIM
    cos = cosq_ref[...]
    sin = sinq_ref[...]
    qs = []
    for gi in range(ATT_GROUP):
        qg = _rope(q_ref[0, :, gi * LANES:(gi + 1) * LANES].astype(F32), cos, sin)
        qs.append(jnp.where(first_half, qg, 0.0))
        qs.append(jnp.where(first_half, 0.0, qg))
    qst = jnp.concatenate(qs, axis=0).astype(ACT)

    jl = j - n_ctx // blk
    start = jnp.clip((jl - 1) * blk, 0, n_lat - n_loc)
    loc = pl.ds(pl.multiple_of(n_ctx + start, blk), n_loc)
    s_ctx = _dot_nt(qst, kr_scr[0:n_ctx, :])
    s_loc = _dot_nt(qst, kr_scr[loc, :])
    q_base = jnp.where(jl >= 0, jl * blk, -(t + WINDOW + blk))
    qpos = q_base + lax.broadcasted_iota(jnp.int32, (blk, n_loc), 0)
    kpos = start + lax.broadcasted_iota(jnp.int32, (blk, n_loc), 1)
    valid = jnp.abs(qpos - kpos) <= WINDOW
    v_ctx = kv_ref[0, 0:n_ctx, LANES:]
    v_loc = kv_ref[0, loc, LANES:]

    halves = []
    for i in range(2 * ATT_GROUP):
        gi, hf = i // 2, i % 2
        sink = sink_ref[hf * ATT_GROUP + gi]
        sc = s_ctx[i * blk:(i + 1) * blk, :]
        sl = jnp.where(valid, s_loc[i * blk:(i + 1) * blk, :], NEG_INF)
        m = jnp.maximum(jnp.maximum(jnp.max(sc, axis=-1, keepdims=True),
                                    jnp.max(sl, axis=-1, keepdims=True)), sink)
        pc = jnp.exp(sc - m)
        pl_ = jnp.exp(sl - m)
        den = (jnp.sum(pc, axis=-1, keepdims=True) + jnp.sum(pl_, axis=-1, keepdims=True)
               + jnp.exp(sink - m))
        o = _dot(pc.astype(ACT), v_ctx) + _dot(pl_.astype(ACT), v_loc)
        halves.append(o / den)
    for gi in range(ATT_GROUP):
        o_ref[0, :, gi * LANES:(gi + 1) * LANES] = jnp.where(
            first_half, halves[2 * gi], halves[2 * gi + 1]).astype(o_ref.dtype)


def _attention(qa, kva, sink, cos, sin_signed, n_ctx):
    bsz, t, _ = qa.shape
    blk = ATT_BLOCK
    return pl.pallas_call(
        functools.partial(_attn_kernel, n_ctx),
        grid=(bsz, t // blk),
        in_specs=[pl.BlockSpec(memory_space=pltpu.SMEM),
                  pl.BlockSpec((1, blk, ATT_QW), lambda b, j: (b, j, 0)),
                  pl.BlockSpec((1, t, 2 * ATT_KVW), lambda b, j: (b, 0, 0)),
                  pl.BlockSpec((blk, LANES), lambda b, j: (j, 0)),
                  pl.BlockSpec((blk, LANES), lambda b, j: (j, 0)),
                  pl.BlockSpec((t, LANES), lambda b, j: (0, 0)),
                  pl.BlockSpec((t, LANES), lambda b, j: (0, 0))],
        out_specs=pl.BlockSpec((1, blk, ATT_QW), lambda b, j: (b, j, 0)),
        out_shape=jax.ShapeDtypeStruct((bsz, t, ATT_QW), ACT),
        scratch_shapes=[pltpu.VMEM((t, LANES), ACT)],
        compiler_params=_cparams(("arbitrary", "arbitrary")),
    )(sink, qa, kva, cos, sin_signed, cos, sin_signed)


def _merge_kernel(n_ctx, bm, ya_ref, yb_ref, yc_ref, ma_ref, mb_ref, mc_ref, x_ref, mod_ref,
                  wa_ref, wb_ref, wc_ref, wo_ref, lnw_ref, lnb_ref, o_ref):
    row0 = pl.program_id(1) * bm
    m = (_sigmoid(ma_ref[0].astype(F32)) * _dot(ya_ref[0], wa_ref[...])
         + _sigmoid(mb_ref[0].astype(F32)) * _dot(yb_ref[0], wb_ref[...])
         + _sigmoid(mc_ref[0].astype(F32)) * _dot(yc_ref[0], wc_ref[...]))
    mix = _dot(m.astype(ACT), wo_ref[...])
    gate = _row_mod(mod_ref[0], 2, row0, bm, n_ctx)
    o_ref[0] = _layer_norm(DEEPNORM_ALPHA * x_ref[0] + gate * mix, lnw_ref[...], lnb_ref[...])


def _merge(ya, yb, yc, gates, xs, modb, wa, wb, wc, wo, lnw, lnb, n_ctx, bm):
    bsz, t, d = xs.shape
    row = lambda width, i=0: pl.BlockSpec((1, bm, width), lambda b, r: (b, r, i))
    const = lambda a: pl.BlockSpec(a.shape, lambda b, r: (0,) * a.ndim)
    return pl.pallas_call(
        functools.partial(_merge_kernel, n_ctx, bm),
        grid=(bsz, t // bm),
        in_specs=[row(GLA_VW), row(CONV_W), row(ATT_QW), row(d, 0), row(d, 1), row(d, 2), row(d),
                  pl.BlockSpec((1, 16, d), lambda b, r: (b, 0, 0)),
                  const(wa), const(wb), const(wc), const(wo), const(lnw), const(lnb)],
        out_specs=row(d),
        out_shape=jax.ShapeDtypeStruct((bsz, t, d), F32),
        compiler_params=_cparams(("arbitrary", "arbitrary")),
    )(ya, yb, yc, gates, gates, gates, xs, modb, wa, wb, wc, wo, lnw, lnb)


def _ffn_kernel(n_ctx, rb, tn, x_ref, xp_ref, xn_ref, mod_ref, wup_ref, cw_ref, wd_ref, lnw_ref, lnb_ref,
                o_ref, h_scr, act_scr):
    t = rb * pl.num_programs(1)
    d_ff = wd_ref.shape[0]
    r = pl.program_id(1)
    row0 = r * rb
    ext = rb + 2 * HALO
    edge = xp_ref.shape[1]
    mod = mod_ref[0]

    def modulated(xv, first_row):
        n = xv.shape[0]
        return xv * (1.0 + _row_mod(mod, 4, first_row, n, n_ctx)) + _row_mod(mod, 3, first_row, n, n_ctx)

    def same_sequence(rows, neighbour):
        in_lat = (rows >= n_ctx).astype(jnp.int32)
        return (rows >= 0) & (rows < t) & (in_lat == (neighbour >= n_ctx).astype(jnp.int32))

    rows_p = row0 - edge + lax.broadcasted_iota(jnp.int32, (edge, 1), 0)
    rows_n = row0 + rb + lax.broadcasted_iota(jnp.int32, (edge, 1), 0)
    hp = jnp.where(same_sequence(rows_p, row0), modulated(xp_ref[0], row0 - edge), 0.0)
    hn = jnp.where(same_sequence(rows_n, row0 + rb - 1), modulated(xn_ref[0], row0 + rb), 0.0)
    zpad = jnp.zeros((HALO - edge, x_ref.shape[2]), F32)
    h_scr[0:HALO, :] = jnp.concatenate([zpad, hp], axis=0).astype(ACT)
    h_scr[HALO:HALO + rb, :] = modulated(x_ref[0], row0).astype(ACT)
    h_scr[HALO + rb:ext, :] = jnp.concatenate([hn, zpad], axis=0).astype(ACT)

    def conv3(u, w, keep_prev=None, keep_next=None):
        prev = pltpu.roll(u, 1, 0)
        nxt = pltpu.roll(u, u.shape[0] - 1, 0)
        if keep_prev is not None:
            prev = jnp.where(keep_prev, prev, 0.0)
            nxt = jnp.where(keep_next, nxt, 0.0)
        return w[0:1, :] * prev + w[1:2, :] * u + w[2:3, :] * nxt

    lb = n_ctx % rb
    fix = HALO + 8
    he = h_scr[...]
    nchunks = d_ff // tn

    def up(n):
        return (_dot(he, wup_ref[:, n * tn:(n + 1) * tn]),
                _dot(he, wup_ref[:, d_ff + n * tn:d_ff + (n + 1) * tn]))

    for n in range(nchunks):
        cg = slice(n * tn, (n + 1) * tn)
        cv = slice(d_ff + n * tn, d_ff + (n + 1) * tn)
        ug, uv = up(n)
        wg = cw_ref[:, cg]
        wv = cw_ref[:, cv]
        gate = conv3(ug, wg)[HALO:HALO + rb, :]
        val = conv3(uv, wv)[HALO:HALO + rb, :]
        act_scr[:, cg] = (_silu(gate) * val).astype(ACT)
        if lb:
            rows_g = row0 + lb - fix + lax.broadcasted_iota(jnp.int32, (2 * fix, 1), 0)
            win = slice(HALO + lb - fix, HALO + lb + fix)
            g_w = conv3(ug[win, :], wg, rows_g != n_ctx, rows_g != n_ctx - 1)
            v_w = conv3(uv[win, :], wv, rows_g != n_ctx, rows_g != n_ctx - 1)
            a_w = _silu(g_w) * v_w
            act_scr[lb - HALO:lb + HALO, cg] = a_w[fix - HALO:fix + HALO, :].astype(ACT)

    y = _dot(act_scr[...], wd_ref[...])
    gate2 = _row_mod(mod, 5, row0, rb, n_ctx)
    o_ref[0] = _layer_norm(DEEPNORM_ALPHA * x_ref[0] + gate2 * y, lnw_ref[...], lnb_ref[...])


def _ffn(xs, modb, w_up, w_conv, w_down, lnw, lnb, n_ctx, rb, tn):
    bsz, t, d = xs.shape
    d_ff = w_down.shape[0]
    edge = 8
    lb = n_ctx % rb
    assert lb == 0 or (2 * HALO <= lb <= rb - 2 * HALO and lb % HALO == 0)
    nb = rb // edge
    resident = lambda a: pl.BlockSpec(a.shape, lambda b, r: (0,) * a.ndim, pipeline_mode=pl.Buffered(1))
    return pl.pallas_call(
        functools.partial(_ffn_kernel, n_ctx, rb, tn),
        grid=(bsz, t // rb),
        in_specs=[pl.BlockSpec((1, rb, d), lambda b, r: (b, r, 0)),
                  pl.BlockSpec((1, edge, d), lambda b, r: (b, jnp.maximum(r * nb - 1, 0), 0)),
                  pl.BlockSpec((1, edge, d), lambda b, r: (b, jnp.minimum((r + 1) * nb, t // edge - 1), 0)),
                  pl.BlockSpec((1, 16, d), lambda b, r: (b, 0, 0)),
                  resident(w_up), resident(w_conv), resident(w_down), resident(lnw), resident(lnb)],
        out_specs=pl.BlockSpec((1, rb, d), lambda b, r: (b, r, 0)),
        out_shape=jax.ShapeDtypeStruct((bsz, t, d), F32),
        scratch_shapes=[pltpu.VMEM((rb + 2 * HALO, d), ACT), pltpu.VMEM((rb, d_ff), ACT)],
        compiler_params=_cparams(("arbitrary", "arbitrary")),
    )(xs, xs, xs, modb, w_up, w_conv, w_down, lnw, lnb)


def _rope_tables(n_ctx, seq):
    rows = seq // GRID_W
    row_idx = np.repeat(np.arange(rows), GRID_W).astype(np.float32)
    col_idx = np.tile(np.arange(GRID_W), rows).astype(np.float32)
    half = HEAD_DIM // 2
    inv_freq = (ROPE_THETA ** (-np.arange(0, half, 2, dtype=np.float32) / half)).astype(np.float32)
    ang_r = jnp.asarray(row_idx[:, None] * inv_freq)
    ang_c = jnp.asarray(col_idx[:, None] * inv_freq)
    cos = jnp.concatenate([jnp.cos(ang_r)] * 2 + [jnp.cos(ang_c)] * 2, axis=-1)
    sin = jnp.concatenate([-jnp.sin(ang_r), jnp.sin(ang_r), -jnp.sin(ang_c), jnp.sin(ang_c)], axis=-1)
    cos = jnp.concatenate([jnp.ones((n_ctx, HEAD_DIM), F32), cos], axis=0)
    sin = jnp.concatenate([jnp.zeros((n_ctx, HEAD_DIM), F32), sin], axis=0)
    return jnp.tile(cos, (1, 2)), jnp.tile(sin, (1, 2))


def _head_perm():
    perm = []
    for gi in range(ATT_GROUP):
        for hf in range(ATT_KV_HEADS):
            head = hf * ATT_GROUP + gi
            perm.extend(range(head * HEAD_DIM, (head + 1) * HEAD_DIM))
    return np.asarray(perm)


def _largest_divisor(n, cap, multiple):
    best = multiple
    for cand in range(multiple, cap + 1, multiple):
        if n % cand == 0:
            best = cand
    return best


def kernel(x, c, ctx, c_ctx, w_ada, b_ada, w_in, gla_gate_up_f, gla_gate_bias_f, gla_gate_up_b,
           gla_gate_bias_b, gla_norm_w, conv_w, att_sink, w_branch_a, w_branch_b, w_branch_c, w_out,
           ln1_w, ln1_b, ffn_up, ffn_conv, ffn_down, ln2_w, ln2_b):
    bsz, seq, d = x.shape
    n_ctx = ctx.shape[1]
    depth = w_in.shape[0]
    t = n_ctx + seq
    assert n_ctx % GLA_CHUNK == 0 and seq % GLA_CHUNK == 0 and seq >= 3 * ATT_BLOCK
    bm = _largest_divisor(t, 384, 128)
    rb = _largest_divisor(t, 768, 128)
    d_ff = ffn_down.shape[1]
    tn = 256

    xs = jnp.concatenate([ctx, x], axis=1)

    cvec = jnp.zeros((16, d), F32).at[:bsz].set(c).at[bsz].set(c_ctx)
    mod = _modulation(cvec, w_ada, b_ada)
    mod_lat = mod[:, :bsz].reshape(depth, bsz, 6, d)
    mod_ctx = jnp.broadcast_to(mod[:, bsz].reshape(depth, 1, 6, d), (depth, bsz, 6, d))
    modb = jnp.concatenate([mod_ctx, mod_lat, jnp.zeros((depth, bsz, 4, d), F32)], axis=2)

    perm = _head_perm()
    o_mem = np.cumsum([0, GLA_KW, GLA_VW, GATE_RANK, GATE_RANK, ATT_KVW, ATT_KVW]).tolist()
    o_rest = (o_mem[-1] + np.cumsum([0, GLA_KW, GLA_VW, CONV_W, CONV_W, CONV_W, ATT_QW, d, d, d])).tolist()
    seg = lambda lo, hi: w_in[:, :, lo:hi]
    qa_w = seg(o_rest[5], o_rest[6])[:, :, perm]
    w_in_r = jnp.concatenate(
        [seg(o_rest[0], o_rest[1]), seg(o_mem[0], o_mem[1]),
         seg(o_mem[1], o_mem[2]),
         seg(o_rest[1], o_rest[2]),
         qa_w,
         seg(o_mem[4], o_mem[6]),
         seg(o_rest[2], o_rest[5]),
         seg(o_rest[6], o_rest[9]),
         seg(o_mem[2], o_mem[4]),
         jnp.zeros((depth, d, _W_GLR - 2 * GATE_RANK), F32)], axis=2).astype(ACT)
    zpad = jnp.zeros((depth, GATE_RANK, GLA_KW), F32)
    w_gate = jnp.concatenate(
        [jnp.concatenate([gla_gate_up_f, zpad], axis=2), jnp.concatenate([zpad, gla_gate_up_b], axis=2),
         jnp.zeros((depth, _W_GLR - 2 * GATE_RANK, 2 * GLA_KW), F32)], axis=1)
    b_gate = jnp.concatenate([gla_gate_bias_f, gla_gate_bias_b], axis=1).reshape(depth, 1, 2 * GLA_KW)

    cos, sin_signed = _rope_tables(n_ctx, seq)
    nw = gla_norm_w.reshape(depth, 1, GLA_DV)
    sink_p = att_sink
    wa = w_branch_a.astype(ACT)
    wb = w_branch_b.astype(ACT)
    wc = w_branch_c[:, perm, :].astype(ACT)
    wo = w_out.astype(ACT)
    w_up = ffn_up.astype(ACT)
    w_dn = ffn_down.astype(ACT)

    for l in range(depth):
        qk, v, r, qa, kva, conv3, gates, g = _inproj(xs, modb[l], w_in_r[l], w_gate[l], b_gate[l], n_ctx, bm)
        ya = _gla(qk, v, r, g, nw[l], n_ctx)
        yb = _short_conv(conv3, conv_w[l], n_ctx)
        yc = _attention(qa, kva, sink_p[l], cos, sin_signed, n_ctx)
        xs = _merge(ya, yb, yc, gates, xs, modb[l], wa[l], wb[l], wc[l], wo[l],
                    ln1_w[l].reshape(1, d), ln1_b[l].reshape(1, d), n_ctx, bm)
        xs = _ffn(xs, modb[l], w_up[l], ffn_conv[l], w_dn[l],
                  ln2_w[l].reshape(1, d), ln2_b[l].reshape(1, d), n_ctx, rb, tn)
    return xs[:, n_ctx:, :]
```

```python
import functools

import numpy as np
import jax
import jax.numpy as jnp
from jax import lax
from jax.experimental import pallas as pl
from jax.experimental.pallas import tpu as pltpu

DEPTH_FOR_DEEPNORM = 4
DEEPNORM_ALPHA = (2 * DEPTH_FOR_DEEPNORM) ** 0.25
LN_EPS = 1e-5
RMS_EPS = 1e-6
GATE_TAU = 16.0
NEG_INF = -1e30
ROPE_THETA = 10000.0
LOG2E = 1.4426950408889634

GRID_W = 64
GLA_HEADS = 4
GLA_DK = 64
GLA_DV = 128
GLA_KW = GLA_HEADS * GLA_DK
GLA_VW = GLA_HEADS * GLA_DV
GATE_RANK = 16
CONV_W = 512
ATT_HEADS = 8
ATT_KV_HEADS = 2
ATT_GROUP = ATT_HEADS // ATT_KV_HEADS
HEAD_DIM = 64
ATT_QW = ATT_HEADS * HEAD_DIM
ATT_KVW = ATT_KV_HEADS * HEAD_DIM
WINDOW = 128
ATT_BLOCK = 128

LANES = 128
GLA_CHUNK = 128
HALO = 16
VMEM_LIMIT = 56 * 1024 * 1024

ACT = jnp.bfloat16
F32 = jnp.float32


def _cparams(sem):
    return pltpu.CompilerParams(dimension_semantics=sem, vmem_limit_bytes=VMEM_LIMIT)


def _split_hi_lo(a):
    hi = a.astype(ACT)
    lo = (a - hi.astype(F32)).astype(ACT)
    return hi, lo


def _dot(a, b):
    return jnp.dot(a, b, preferred_element_type=F32)


def _dot_nt(a, b):
    return lax.dot_general(a, b, (((1,), (1,)), ((), ())), preferred_element_type=F32)


def _dot3(a, b):
    ah, al = _split_hi_lo(a)
    bh, bl = _split_hi_lo(b)
    return _dot(ah, bh) + _dot(ah, bl) + _dot(al, bh)


def _sigmoid(x):
    return 1.0 / (1.0 + jnp.exp(-x))


def _silu(x):
    return x * _sigmoid(x)


def _log_sigmoid(x):
    return jnp.minimum(x, 0.0) - jnp.log(1.0 + jnp.exp(-jnp.abs(x)))


def _layer_norm(y, w, b):
    mu = jnp.mean(y, axis=-1, keepdims=True)
    yc = y - mu
    var = jnp.mean(yc * yc, axis=-1, keepdims=True)
    return yc * lax.rsqrt(var + LN_EPS) * w + b


def _row_mod(mod, idx, row0, nrows, n_ctx):
    rows = row0 + lax.broadcasted_iota(jnp.int32, (nrows, 1), 0)
    return jnp.where(rows < n_ctx, mod[idx:idx + 1, :], mod[6 + idx:7 + idx, :])


def _mod_kernel(c_ref, w_ref, b_ref, o_ref):
    s = _silu(c_ref[...])
    o_ref[0] = _dot3(s, w_ref[0]) + b_ref[0]


def _modulation(cvec, w_ada, b_ada):
    depth, d, n = w_ada.shape
    tn = n // 4
    rows = cvec.shape[0]
    return pl.pallas_call(
        _mod_kernel,
        grid=(depth, n // tn),
        in_specs=[pl.BlockSpec((rows, d), lambda l, j: (0, 0)),
                  pl.BlockSpec((1, d, tn), lambda l, j: (l, 0, j)),
                  pl.BlockSpec((1, 1, tn), lambda l, j: (l, 0, j))],
        out_specs=pl.BlockSpec((1, rows, tn), lambda l, j: (l, 0, j)),
        out_shape=jax.ShapeDtypeStruct((depth, rows, n), F32),
        compiler_params=_cparams(("arbitrary", "arbitrary")),
    )(cvec, w_ada, b_ada.reshape(depth, 1, n))


_W_K, _W_V, _W_KVA, _W_Q, _W_R, _W_CONV, _W_QA, _W_GATES, _W_GLR = 256, 512, 256, 256, 512, 1536, 512, 3072, 128
_OFF = np.cumsum([0, _W_K, _W_V, _W_KVA, _W_Q, _W_R, _W_CONV, _W_QA, _W_GATES, _W_GLR]).tolist()


def _inproj_kernel(n_ctx, bm, nsub, x_ref, mod_ref, w_ref, wg_ref, bg_ref,
                   k_ref, v_ref, kva_ref, q_ref, r_ref, conv_ref, qa_ref, gates_ref, g_ref):
    mod = mod_ref[0]
    sub_rows = bm // nsub
    for sub in range(nsub):
        rows = slice(sub * sub_rows, (sub + 1) * sub_rows)
        row0 = pl.program_id(1) * bm + sub * sub_rows
        sh = _row_mod(mod, 0, row0, sub_rows, n_ctx)
        sc = _row_mod(mod, 1, row0, sub_rows, n_ctx)
        h = (x_ref[0, rows, :] * (1.0 + sc) + sh).astype(ACT)

        def proj(c0, width):
            return _dot(h, w_ref[:, c0:c0 + width])

        z = _dot3(proj(_OFF[8], _W_GLR), wg_ref[...]) + bg_ref[...]
        g_ref[0, rows, :] = _log_sigmoid(z) * (1.0 / GATE_TAU)
        k_ref[0, rows, :] = proj(_OFF[0], _W_K).astype(ACT)
        v_ref[0, rows, :] = proj(_OFF[1], _W_V).astype(ACT)
        kva_ref[0, rows, :] = proj(_OFF[2], _W_KVA).astype(ACT)
        q_ref[0, rows, :] = (proj(_OFF[3], _W_Q) * (GLA_DK ** -0.5)).astype(ACT)
        r_ref[0, rows, :] = proj(_OFF[4], _W_R).astype(ACT)
        for i in range(_W_CONV // 512):
            conv_ref[0, rows, i * 512:(i + 1) * 512] = proj(_OFF[5] + i * 512, 512).astype(ACT)
        qa_ref[0, rows, :] = (proj(_OFF[6], _W_QA) * (HEAD_DIM ** -0.5)).astype(ACT)
        for i in range(_W_GATES // 1024):
            gates_ref[0, rows, i * 1024:(i + 1) * 1024] = proj(_OFF[7] + i * 1024, 1024).astype(ACT)


def _inproj(xs, modb, w, wg, bg, n_ctx, bm, nsub):
    bsz, t, d = xs.shape
    row = lambda width: pl.BlockSpec((1, bm, width), lambda b, r: (b, r, 0))
    out = lambda width, dt: jax.ShapeDtypeStruct((bsz, t, width), dt)
    resident = lambda a: pl.BlockSpec(a.shape, lambda b, r: (0,) * a.ndim, pipeline_mode=pl.Buffered(1))
    return pl.pallas_call(
        functools.partial(_inproj_kernel, n_ctx, bm, nsub),
        grid=(bsz, t // bm),
        in_specs=[row(d),
                  pl.BlockSpec((1, 16, d), lambda b, r: (b, 0, 0)),
                  resident(w), resident(wg), resident(bg)],
        out_specs=[row(_W_K), row(_W_V), row(_W_KVA), row(_W_Q), row(_W_R), row(_W_CONV), row(_W_QA),
                   row(_W_GATES), row(2 * GLA_KW)],
        out_shape=[out(_W_K, ACT), out(_W_V, ACT), out(_W_KVA, ACT), out(_W_Q, ACT), out(_W_R, ACT),
                   out(_W_CONV, ACT), out(_W_QA, ACT), out(_W_GATES, ACT), out(2 * GLA_KW, F32)],
        compiler_params=_cparams(("arbitrary", "arbitrary")),
    )(xs, modb, w, wg, bg)


def _gla_kernel(n_ctx, q_ref, k_ref, v_ref, r_ref, g_ref, nw_ref, o_ref, b_scr, u_scr, dec_scr):
    t = q_ref.shape[1]
    L = GLA_CHUNK
    nch = t // L
    nch_ctx = n_ctx // L
    kw = GLA_KW

    ri = lax.broadcasted_iota(jnp.int32, (L, L), 0)
    ci = lax.broadcasted_iota(jnp.int32, (L, L), 1)
    tri_f = (ci <= ri).astype(ACT)
    tri_b = (ci >= ri).astype(ACT)
    lane_k = lax.broadcasted_iota(jnp.int32, (1, kw), 1)
    head_masks = [(lane_k >= h * GLA_DK) & (lane_k < (h + 1) * GLA_DK) for h in range(GLA_HEADS)]

    def rows(c):
        return pl.ds(pl.multiple_of(c * L, L), L)

    def pass1(c, carry):
        rs = rows(c)
        g = g_ref[0, rs, :]
        g_hi, g_lo = _split_hi_lo(g)
        b_f = _dot(tri_f, g_hi[:, :kw]) + _dot(tri_f, g_lo[:, :kw])
        b_b = _dot(tri_b, g_hi[:, kw:]) + _dot(tri_b, g_lo[:, kw:])
        b_scr[rs, :kw] = b_f
        b_scr[rs, kw:] = b_b
        tot_f = b_f[L - 1:L, :]
        tot_b = b_b[0:1, :]
        k = k_ref[0, rs, :].astype(F32)
        ku_f = (k * jnp.exp(tot_f - b_f)).astype(ACT)
        ku_b = (k * jnp.exp(tot_b - b_b)).astype(ACT)
        ku = jnp.concatenate([ku_f, ku_b], axis=1)
        vt = v_ref[0, rs, :].astype(F32).T.astype(ACT)
        u_full = _dot(vt, ku)
        u_f = jnp.zeros((GLA_DV, kw), F32)
        u_b = jnp.zeros((GLA_DV, kw), F32)
        for h in range(GLA_HEADS):
            blk = u_full[h * GLA_DV:(h + 1) * GLA_DV, :]
            u_f = u_f + jnp.where(head_masks[h], blk[:, :kw], 0.0)
            u_b = u_b + jnp.where(head_masks[h], blk[:, kw:], 0.0)
        u_scr[c, 0] = u_f
        u_scr[c, 1] = u_b
        dec_scr[c, 0:1, :] = jnp.exp(tot_f)
        dec_scr[c, 1:2, :] = jnp.exp(tot_b)
        return carry

    unroll = 6 if nch % 6 == 0 else 1
    lax.fori_loop(0, nch, pass1, 0, unroll=unroll)

    def scan_f(c, s):
        u = u_scr[c, 0]
        u_scr[c, 0] = s
        return s * dec_scr[c, 0:1, :] + u

    lax.fori_loop(0, nch, scan_f, jnp.zeros((GLA_DV, kw), F32))

    def scan_b(i, s):
        c = jnp.where(i < nch_ctx, nch_ctx - 1 - i, nch - 1 - (i - nch_ctx))
        u = u_scr[c, 1]
        u_scr[c, 1] = s
        return s * dec_scr[c, 1:2, :] + u

    lax.fori_loop(0, nch, scan_b, jnp.zeros((GLA_DV, kw), F32))

    ri_st = lax.broadcasted_iota(jnp.int32, (GLA_HEADS * L, L), 0) & (L - 1)
    ci_st = lax.broadcasted_iota(jnp.int32, (GLA_HEADS * L, L), 1)
    tril = ci_st <= ri_st
    triu = ci_st >= ri_st
    nw = nw_ref[...]

    def pass2(c, carry):
        rs = rows(c)
        q = q_ref[0, rs, :].astype(F32)
        k = k_ref[0, rs, :].astype(F32)
        v = v_ref[0, rs, :]
        b_f = b_scr[rs, :kw]
        b_b = b_scr[rs, kw:]
        e_f = jnp.exp(b_f)
        e_b = jnp.exp(b_b)
        qd_f = q * e_f
        qd_b = q * e_b
        ki_f = (k / e_f).astype(ACT)
        ki_b = (k / e_b).astype(ACT)
        qs_f = jnp.concatenate([jnp.where(m, qd_f, 0.0) for m in head_masks], axis=0).astype(ACT)
        qs_b = jnp.concatenate([jnp.where(m, qd_b, 0.0) for m in head_masks], axis=0).astype(ACT)
        p = (jnp.where(tril, _dot_nt(qs_f, ki_f), 0.0)
             + jnp.where(triu, _dot_nt(qs_b, ki_b), 0.0)).astype(ACT)
        s_f = u_scr[c, 0].astype(ACT)
        s_b = u_scr[c, 1].astype(ACT)
        inter = _dot_nt(qs_f, s_f) + _dot_nt(qs_b, s_b)
        outs = []
        for h in range(GLA_HEADS):
            o_h = inter[h * L:(h + 1) * L, :] + _dot(p[h * L:(h + 1) * L, :], v[:, h * GLA_DV:(h + 1) * GLA_DV])
            ms = jnp.mean(o_h * o_h, axis=-1, keepdims=True)
            outs.append(o_h * lax.rsqrt(ms + RMS_EPS) * nw)
        o = jnp.concatenate(outs, axis=1)
        o_ref[0, rs, :] = (o * _silu(r_ref[0, rs, :].astype(F32))).astype(o_ref.dtype)
        return carry

    lax.fori_loop(0, nch, pass2, 0, unroll=unroll)


def _gla(q, k, v, r, g, nw, n_ctx):
    bsz, t, _ = q.shape
    nch = t // GLA_CHUNK
    full = lambda width: pl.BlockSpec((1, t, width), lambda b: (b, 0, 0))
    return pl.pallas_call(
        functools.partial(_gla_kernel, n_ctx),
        grid=(bsz,),
        in_specs=[full(GLA_KW), full(GLA_KW), full(GLA_VW), full(GLA_VW), full(2 * GLA_KW),
                  pl.BlockSpec((1, GLA_DV), lambda b: (0, 0))],
        out_specs=full(GLA_VW),
        out_shape=jax.ShapeDtypeStruct((bsz, t, GLA_VW), ACT),
        scratch_shapes=[pltpu.VMEM((t, 2 * GLA_KW), F32),
                        pltpu.VMEM((nch, 2, GLA_DV, GLA_KW), F32),
                        pltpu.VMEM((nch, 8, GLA_KW), F32)],
        compiler_params=_cparams(("arbitrary",)),
    )(q, k, v, r, g, nw)


def _conv_kernel(n_ctx, h_ref, gb_ref, gc_ref, w_ref, o_ref):
    t = h_ref.shape[1]
    a = gc_ref[0].astype(F32) * h_ref[0].astype(F32)
    rows = lax.broadcasted_iota(jnp.int32, (t, 1), 0)
    prev = jnp.where((rows == 0) | (rows == n_ctx), 0.0, pltpu.roll(a, 1, 0))
    nxt = jnp.where((rows == n_ctx - 1) | (rows == t - 1), 0.0, pltpu.roll(a, t - 1, 0))
    w = w_ref[...]
    y = w[0:1, :] * prev + w[1:2, :] * a + w[2:3, :] * nxt
    o_ref[0] = (gb_ref[0].astype(F32) * y).astype(o_ref.dtype)


def _short_conv(conv3, w, n_ctx):
    bsz, t, _ = conv3.shape
    spec = lambda i: pl.BlockSpec((1, t, CONV_W), lambda b: (b, 0, i))
    return pl.pallas_call(
        functools.partial(_conv_kernel, n_ctx),
        grid=(bsz,),
        in_specs=[spec(0), spec(1), spec(2), pl.BlockSpec(w.shape, lambda b: (0, 0))],
        out_specs=pl.BlockSpec((1, t, CONV_W), lambda b: (b, 0, 0)),
        out_shape=jax.ShapeDtypeStruct((bsz, t, CONV_W), ACT),
        compiler_params=_cparams(("arbitrary",)),
    )(conv3, conv3, conv3, w)


def _rope(x, cos, sin_signed):
    lane = lax.broadcasted_iota(jnp.int32, (1, LANES), 1)
    quarter = HEAD_DIM // 4
    swapped = jnp.where((lane % (2 * quarter)) < quarter,
                        pltpu.roll(x, LANES - quarter, 1), pltpu.roll(x, quarter, 1))
    return x * cos + swapped * sin_signed


def _attn_kernel(n_ctx, nb, sink_ref, q_ref, kv_ref, cosq_ref, sinq_ref, cos_all_ref, sin_all_ref, o_ref,
                 kd_scr, vd_scr, s_scr, p_scr):
    step = pl.program_id(1)

    @pl.when(step == 0)
    def _():
        lane = lax.broadcasted_iota(jnp.int32, (1, LANES), 1)
        first_half = lane < HEAD_DIM
        k = _rope(kv_ref[0, :, :LANES].astype(F32), cos_all_ref[...], sin_all_ref[...])
        v = kv_ref[0, :, LANES:].astype(F32)
        for src, dst in ((k, kd_scr), (v, vd_scr)):
            swapped = pltpu.roll(src, HEAD_DIM, 1)
            dst[0] = jnp.where(first_half, src, swapped).astype(ACT)
            dst[1] = jnp.where(first_half, swapped, src).astype(ACT)

    for sub in range(nb):
        _attn_block(n_ctx, step * nb + sub, sub, sink_ref, q_ref, cosq_ref, sinq_ref, o_ref, kd_scr, vd_scr,
                    s_scr.at[sub], p_scr.at[sub])


def _attn_block(n_ctx, j, sub, sink_ref, q_ref, cosq_ref, sinq_ref, o_ref, kd_scr, vd_scr, s_scr, p_scr):
    t = kd_scr.shape[1]
    blk = ATT_BLOCK
    n_loc = 3 * blk
    n_lat = t - n_ctx
    qrows = slice(sub * blk, (sub + 1) * blk)

    lane = lax.broadcasted_iota(jnp.int32, (1, LANES), 1)
    first_half = lane < HEAD_DIM
    cos = cosq_ref[qrows, :]
    sin = sinq_ref[qrows, :]
    qs = []
    for gi in range(ATT_HEADS // 2):
        qg = _rope(q_ref[0, qrows, gi * LANES:(gi + 1) * LANES].astype(F32), cos, sin)
        qs.append(jnp.where(first_half, qg, 0.0))
        qs.append(jnp.where(first_half, 0.0, qg))
    qst = jnp.concatenate(qs, axis=0).astype(ACT)

    jl = j - n_ctx // blk
    start = jnp.clip((jl - 1) * blk, 0, n_lat - n_loc)
    loc = pl.ds(pl.multiple_of(n_ctx + start, blk), n_loc)
    hrows = ATT_GROUP * blk
    for kv in range(ATT_KV_HEADS):
        rs = slice(kv * hrows, (kv + 1) * hrows)
        s_scr[rs, 0:n_ctx] = _dot_nt(qst[rs, :], kd_scr[kv, 0:n_ctx, :])
        s_scr[rs, n_ctx:] = _dot_nt(qst[rs, :], kd_scr[kv, loc, :])
    q_base = jnp.where(jl >= 0, jl * blk, -(t + WINDOW + blk))
    qpos = q_base + lax.broadcasted_iota(jnp.int32, (blk, n_loc), 0)
    kpos = start + lax.broadcasted_iota(jnp.int32, (blk, n_loc), 1)
    bias = jnp.where(jnp.abs(qpos - kpos) <= WINDOW, 0.0, NEG_INF)
    ones = jnp.ones((n_ctx + n_loc, LANES), ACT)

    sink_exp = []
    for i in range(ATT_HEADS):
        rows = slice(i * blk, (i + 1) * blk)
        sink = sink_ref[i] * LOG2E
        sc = s_scr[rows, 0:n_ctx]
        sl = s_scr[rows, n_ctx:] + bias
        m = jnp.maximum(jnp.maximum(jnp.max(sc, axis=-1, keepdims=True),
                                    jnp.max(sl, axis=-1, keepdims=True)), sink)
        p_scr[rows, 0:n_ctx] = jnp.exp2(sc - m).astype(ACT)
        p_scr[rows, n_ctx:] = jnp.exp2(sl - m).astype(ACT)
        sink_exp.append(jnp.exp2(sink - m))
    outs = []
    for kv in range(ATT_KV_HEADS):
        rs = slice(kv * hrows, (kv + 1) * hrows)
        v_aug = jnp.concatenate(
            [jnp.concatenate([vd_scr[kv, 0:n_ctx, :], vd_scr[kv, loc, :]], axis=0), ones], axis=1)
        o_kv = _dot(p_scr[rs, :], v_aug)
        for g in range(ATT_GROUP):
            i = kv * ATT_GROUP + g
            rows = slice(g * blk, (g + 1) * blk)
            outs.append(o_kv[rows, :LANES] / (o_kv[rows, LANES:] + sink_exp[i]))
    for gi in range(ATT_HEADS // 2):
        o_ref[0, qrows, gi * LANES:(gi + 1) * LANES] = jnp.where(
            first_half, outs[2 * gi], outs[2 * gi + 1]).astype(o_ref.dtype)


def _attention(qa, kva, sink, cos, sin_signed, n_ctx, nb):
    bsz, t, _ = qa.shape
    blk = ATT_BLOCK
    rows = nb * blk
    return pl.pallas_call(
        functools.partial(_attn_kernel, n_ctx, nb),
        grid=(bsz, t // rows),
        in_specs=[pl.BlockSpec(memory_space=pltpu.SMEM),
                  pl.BlockSpec((1, rows, ATT_QW), lambda b, j: (b, j, 0)),
                  pl.BlockSpec((1, t, 2 * ATT_KVW), lambda b, j: (b, 0, 0)),
                  pl.BlockSpec((rows, LANES), lambda b, j: (j, 0)),
                  pl.BlockSpec((rows, LANES), lambda b, j: (j, 0)),
                  pl.BlockSpec((t, LANES), lambda b, j: (0, 0)),
                  pl.BlockSpec((t, LANES), lambda b, j: (0, 0))],
        out_specs=pl.BlockSpec((1, rows, ATT_QW), lambda b, j: (b, j, 0)),
        out_shape=jax.ShapeDtypeStruct((bsz, t, ATT_QW), ACT),
        scratch_shapes=[pltpu.VMEM((ATT_KV_HEADS, t, LANES), ACT),
                        pltpu.VMEM((ATT_KV_HEADS, t, LANES), ACT),
                        pltpu.VMEM((nb, ATT_HEADS * blk, n_ctx + 3 * blk), F32),
                        pltpu.VMEM((nb, ATT_HEADS * blk, n_ctx + 3 * blk), ACT)],
        compiler_params=_cparams(("arbitrary", "arbitrary")),
    )(sink, qa, kva, cos * LOG2E, sin_signed * LOG2E, cos, sin_signed)


def _merge_kernel(n_ctx, bm, nsub, ya_ref, yb_ref, yc_ref, ma_ref, mb_ref, mc_ref, x_ref, mod_ref,
                  wa_ref, wb_ref, wc_ref, wo_ref, lnw_ref, lnb_ref, o_ref):
    sub_rows = bm // nsub
    for sub in range(nsub):
        rows = slice(sub * sub_rows, (sub + 1) * sub_rows)
        row0 = pl.program_id(1) * bm + sub * sub_rows
        m = (_sigmoid(ma_ref[0, rows, :].astype(F32)) * _dot(ya_ref[0, rows, :], wa_ref[...])
             + _sigmoid(mb_ref[0, rows, :].astype(F32)) * _dot(yb_ref[0, rows, :], wb_ref[...])
             + _sigmoid(mc_ref[0, rows, :].astype(F32)) * _dot(yc_ref[0, rows, :], wc_ref[...]))
        mix = _dot(m.astype(ACT), wo_ref[...])
        gate = _row_mod(mod_ref[0], 2, row0, sub_rows, n_ctx)
        o_ref[0, rows, :] = _layer_norm(DEEPNORM_ALPHA * x_ref[0, rows, :] + gate * mix,
                                        lnw_ref[...], lnb_ref[...])


def _merge(ya, yb, yc, gates, xs, modb, wa, wb, wc, wo, lnw, lnb, n_ctx, bm, nsub):
    bsz, t, d = xs.shape
    row = lambda width, i=0: pl.BlockSpec((1, bm, width), lambda b, r: (b, r, i))
    const = lambda a: pl.BlockSpec(a.shape, lambda b, r: (0,) * a.ndim, pipeline_mode=pl.Buffered(1))
    return pl.pallas_call(
        functools.partial(_merge_kernel, n_ctx, bm, nsub),
        grid=(bsz, t // bm),
        in_specs=[row(GLA_VW), row(CONV_W), row(ATT_QW), row(d, 0), row(d, 1), row(d, 2), row(d),
                  pl.BlockSpec((1, 16, d), lambda b, r: (b, 0, 0)),
                  const(wa), const(wb), const(wc), const(wo), const(lnw), const(lnb)],
        out_specs=row(d),
        out_shape=jax.ShapeDtypeStruct((bsz, t, d), F32),
        compiler_params=_cparams(("arbitrary", "arbitrary")),
    )(ya, yb, yc, gates, gates, gates, xs, modb, wa, wb, wc, wo, lnw, lnb)


def _ffn_kernel(n_ctx, rb, tn, x_ref, xp_ref, xn_ref, mod_ref, wup_ref, cw_ref, wd_ref, lnw_ref, lnb_ref,
                o_ref, h_scr, act_scr):
    t = rb * pl.num_programs(1)
    d_ff = wd_ref.shape[0]
    r = pl.program_id(1)
    row0 = r * rb
    ext = rb + 2 * HALO
    edge = xp_ref.shape[1]
    mod = mod_ref[0]

    def modulated(xv, first_row):
        n = xv.shape[0]
        return xv * (1.0 + _row_mod(mod, 4, first_row, n, n_ctx)) + _row_mod(mod, 3, first_row, n, n_ctx)

    def same_sequence(rows, neighbour):
        in_lat = (rows >= n_ctx).astype(jnp.int32)
        return (rows >= 0) & (rows < t) & (in_lat == (neighbour >= n_ctx).astype(jnp.int32))

    rows_p = row0 - edge + lax.broadcasted_iota(jnp.int32, (edge, 1), 0)
    rows_n = row0 + rb + lax.broadcasted_iota(jnp.int32, (edge, 1), 0)
    hp = jnp.where(same_sequence(rows_p, row0), modulated(xp_ref[0], row0 - edge), 0.0)
    hn = jnp.where(same_sequence(rows_n, row0 + rb - 1), modulated(xn_ref[0], row0 + rb), 0.0)
    zpad = jnp.zeros((HALO - edge, x_ref.shape[2]), F32)
    h_scr[0:HALO, :] = jnp.concatenate([zpad, hp], axis=0).astype(ACT)
    h_scr[HALO:HALO + rb, :] = modulated(x_ref[0], row0).astype(ACT)
    h_scr[HALO + rb:ext, :] = jnp.concatenate([hn, zpad], axis=0).astype(ACT)

    def conv3(u, w, keep_prev=None, keep_next=None):
        prev = pltpu.roll(u, 1, 0)
        nxt = pltpu.roll(u, u.shape[0] - 1, 0)
        if keep_prev is not None:
            prev = jnp.where(keep_prev, prev, 0.0)
            nxt = jnp.where(keep_next, nxt, 0.0)
        return w[0:1, :] * prev + w[1:2, :] * u + w[2:3, :] * nxt

    lb = n_ctx % rb
    fix = HALO + 8
    he = h_scr[...]
    nchunks = d_ff // tn

    def up(n):
        return (_dot(he, wup_ref[:, n * tn:(n + 1) * tn]),
                _dot(he, wup_ref[:, d_ff + n * tn:d_ff + (n + 1) * tn]))

    for n in range(nchunks):
        cg = slice(n * tn, (n + 1) * tn)
        cv = slice(d_ff + n * tn, d_ff + (n + 1) * tn)
        ug, uv = up(n)
        wg = cw_ref[:, cg]
        wv = cw_ref[:, cv]
        gate = conv3(ug, wg)[HALO:HALO + rb, :]
        val = conv3(uv, wv)[HALO:HALO + rb, :]
        act_scr[:, cg] = (_silu(gate) * val).astype(ACT)
        if lb:
            rows_g = row0 + lb - fix + lax.broadcasted_iota(jnp.int32, (2 * fix, 1), 0)
            win = slice(HALO + lb - fix, HALO + lb + fix)
            g_w = conv3(ug[win, :], wg, rows_g != n_ctx, rows_g != n_ctx - 1)
            v_w = conv3(uv[win, :], wv, rows_g != n_ctx, rows_g != n_ctx - 1)
            a_w = _silu(g_w) * v_w
            act_scr[lb - HALO:lb + HALO, cg] = a_w[fix - HALO:fix + HALO, :].astype(ACT)

    y = _dot(act_scr[...], wd_ref[...])
    gate2 = _row_mod(mod, 5, row0, rb, n_ctx)
    o_ref[0] = _layer_norm(DEEPNORM_ALPHA * x_ref[0] + gate2 * y, lnw_ref[...], lnb_ref[...])


def _ffn(xs, modb, w_up, w_conv, w_down, lnw, lnb, n_ctx, rb, tn):
    bsz, t, d = xs.shape
    d_ff = w_down.shape[0]
    edge = 8
    lb = n_ctx % rb
    assert lb == 0 or (2 * HALO <= lb <= rb - 2 * HALO and lb % HALO == 0)
    nb = rb // edge
    resident = lambda a: pl.BlockSpec(a.shape, lambda b, r: (0,) * a.ndim, pipeline_mode=pl.Buffered(1))
    return pl.pallas_call(
        functools.partial(_ffn_kernel, n_ctx, rb, tn),
        grid=(bsz, t // rb),
        in_specs=[pl.BlockSpec((1, rb, d), lambda b, r: (b, r, 0)),
                  pl.BlockSpec((1, edge, d), lambda b, r: (b, jnp.maximum(r * nb - 1, 0), 0)),
                  pl.BlockSpec((1, edge, d), lambda b, r: (b, jnp.minimum((r + 1) * nb, t // edge - 1), 0)),
                  pl.BlockSpec((1, 16, d), lambda b, r: (b, 0, 0)),
                  resident(w_up), resident(w_conv), resident(w_down), resident(lnw), resident(lnb)],
        out_specs=pl.BlockSpec((1, rb, d), lambda b, r: (b, r, 0)),
        out_shape=jax.ShapeDtypeStruct((bsz, t, d), F32),
        scratch_shapes=[pltpu.VMEM((rb + 2 * HALO, d), ACT), pltpu.VMEM((rb, d_ff), ACT)],
        compiler_params=_cparams(("arbitrary", "arbitrary")),
    )(xs, xs, xs, modb, w_up, w_conv, w_down, lnw, lnb)


def _rope_tables(n_ctx, seq):
    rows = seq // GRID_W
    row_idx = np.repeat(np.arange(rows), GRID_W).astype(np.float32)
    col_idx = np.tile(np.arange(GRID_W), rows).astype(np.float32)
    half = HEAD_DIM // 2
    inv_freq = (ROPE_THETA ** (-np.arange(0, half, 2, dtype=np.float32) / half)).astype(np.float32)
    ang_r = jnp.asarray(row_idx[:, None] * inv_freq)
    ang_c = jnp.asarray(col_idx[:, None] * inv_freq)
    cos = jnp.concatenate([jnp.cos(ang_r)] * 2 + [jnp.cos(ang_c)] * 2, axis=-1)
    sin = jnp.concatenate([-jnp.sin(ang_r), jnp.sin(ang_r), -jnp.sin(ang_c), jnp.sin(ang_c)], axis=-1)
    cos = jnp.concatenate([jnp.ones((n_ctx, HEAD_DIM), F32), cos], axis=0)
    sin = jnp.concatenate([jnp.zeros((n_ctx, HEAD_DIM), F32), sin], axis=0)
    return jnp.tile(cos, (1, 2)), jnp.tile(sin, (1, 2))


def _largest_divisor(n, cap, multiple):
    best = multiple
    for cand in range(multiple, cap + 1, multiple):
        if n % cand == 0:
            best = cand
    return best


def kernel(x, c, ctx, c_ctx, w_ada, b_ada, w_in, gla_gate_up_f, gla_gate_bias_f, gla_gate_up_b,
           gla_gate_bias_b, gla_norm_w, conv_w, att_sink, w_branch_a, w_branch_b, w_branch_c, w_out,
           ln1_w, ln1_b, ffn_up, ffn_conv, ffn_down, ln2_w, ln2_b):
    bsz, seq, d = x.shape
    n_ctx = ctx.shape[1]
    depth = w_in.shape[0]
    t = n_ctx + seq
    assert n_ctx % GLA_CHUNK == 0 and seq % GLA_CHUNK == 0 and seq >= 3 * ATT_BLOCK
    rb = _largest_divisor(t, 768, 128)
    bm = rb
    nsub = 2 if bm % 256 == 0 else 1
    d_ff = ffn_down.shape[1]
    tn = 256
    nb_att = 2 if (t // ATT_BLOCK) % 2 == 0 else 1

    xs = jnp.concatenate([ctx, x], axis=1)

    cvec = jnp.zeros((16, d), F32).at[:bsz].set(c).at[bsz].set(c_ctx)
    mod = _modulation(cvec, w_ada, b_ada)
    mod_lat = mod[:, :bsz].reshape(depth, bsz, 6, d)
    mod_ctx = jnp.broadcast_to(mod[:, bsz].reshape(depth, 1, 6, d), (depth, bsz, 6, d))
    modb = jnp.concatenate([mod_ctx, mod_lat, jnp.zeros((depth, bsz, 4, d), F32)], axis=2)

    glr0 = GLA_KW + GLA_VW
    glr1 = glr0 + 2 * GATE_RANK
    assert w_in.shape[2] - 2 * GATE_RANK == _OFF[-2]
    w_in_r = jnp.concatenate(
        [w_in[:, :, :glr0], w_in[:, :, glr1:], w_in[:, :, glr0:glr1],
         jnp.zeros((depth, d, _W_GLR - 2 * GATE_RANK), F32)], axis=2).astype(ACT)
    zpad = jnp.zeros((depth, GATE_RANK, GLA_KW), F32)
    w_gate = jnp.concatenate(
        [jnp.concatenate([gla_gate_up_f, zpad], axis=2), jnp.concatenate([zpad, gla_gate_up_b], axis=2),
         jnp.zeros((depth, _W_GLR - 2 * GATE_RANK, 2 * GLA_KW), F32)], axis=1)
    b_gate = jnp.concatenate([gla_gate_bias_f, gla_gate_bias_b], axis=1).reshape(depth, 1, 2 * GLA_KW)

    cos, sin_signed = _rope_tables(n_ctx, seq)
    nw = gla_norm_w.reshape(depth, 1, GLA_DV)
    wa = w_branch_a.astype(ACT)
    wb = w_branch_b.astype(ACT)
    wc = w_branch_c.astype(ACT)
    wo = w_out.astype(ACT)
    w_up = ffn_up.astype(ACT)
    w_dn = ffn_down.astype(ACT)

    for l in range(depth):
        k, v, kva, q, r, conv3, qa, gates, g = _inproj(xs, modb[l], w_in_r[l], w_gate[l], b_gate[l], n_ctx,
                                                       bm, nsub)
        ya = _gla(q, k, v, r, g, nw[l], n_ctx)
        yb = _short_conv(conv3, conv_w[l], n_ctx)
        yc = _attention(qa, kva, att_sink[l], cos, sin_signed, n_ctx, nb_att)
        xs = _merge(ya, yb, yc, gates, xs, modb[l], wa[l], wb[l], wc[l], wo[l],
                    ln1_w[l].reshape(1, d), ln1_b[l].reshape(1, d), n_ctx, bm, nsub)
        xs = _ffn(xs, modb[l], w_up[l], ffn_conv[l], w_dn[l],
                  ln2_w[l].reshape(1, d), ln2_b[l].reshape(1, d), n_ctx, rb, tn)
    return xs[:, n_ctx:, :]
```

```python
import functools

import numpy as np
import jax
import jax.numpy as jnp
from jax import lax
from jax.experimental import pallas as pl
from jax.experimental.pallas import tpu as pltpu

DEPTH_FOR_DEEPNORM = 4
DEEPNORM_ALPHA = (2 * DEPTH_FOR_DEEPNORM) ** 0.25
LN_EPS = 1e-5
RMS_EPS = 1e-6
GATE_TAU = 16.0
NEG_INF = -1e30
ROPE_THETA = 10000.0
LOG2E = 1.4426950408889634

GRID_W = 64
GLA_HEADS = 4
GLA_DK = 64
GLA_DV = 128
GLA_KW = GLA_HEADS * GLA_DK
GLA_VW = GLA_HEADS * GLA_DV
GATE_RANK = 16
CONV_W = 512
ATT_HEADS = 8
ATT_KV_HEADS = 2
ATT_GROUP = ATT_HEADS // ATT_KV_HEADS
HEAD_DIM = 64
ATT_QW = ATT_HEADS * HEAD_DIM
ATT_KVW = ATT_KV_HEADS * HEAD_DIM
WINDOW = 128
ATT_BLOCK = 128

LANES = 128
GLA_CHUNK = 128
HALO = 16
VMEM_LIMIT = 56 * 1024 * 1024

ACT = jnp.bfloat16
F32 = jnp.float32


def _cparams(sem):
    return pltpu.CompilerParams(dimension_semantics=sem, vmem_limit_bytes=VMEM_LIMIT)


def _split_hi_lo(a):
    hi = a.astype(ACT)
    lo = (a - hi.astype(F32)).astype(ACT)
    return hi, lo


def _dot(a, b):
    return jnp.dot(a, b, preferred_element_type=F32)


def _dot_nt(a, b):
    return lax.dot_general(a, b, (((1,), (1,)), ((), ())), preferred_element_type=F32)


def _dot3(a, b):
    ah, al = _split_hi_lo(a)
    bh, bl = _split_hi_lo(b)
    return _dot(ah, bh) + _dot(ah, bl) + _dot(al, bh)


def _sigmoid(x):
    return 1.0 / (1.0 + jnp.exp(-x))


def _silu(x):
    return x * _sigmoid(x)


def _log_sigmoid(x):
    return jnp.minimum(x, 0.0) - jnp.log(1.0 + jnp.exp(-jnp.abs(x)))


def _layer_norm(y, w, b):
    mu = jnp.mean(y, axis=-1, keepdims=True)
    yc = y - mu
    var = jnp.mean(yc * yc, axis=-1, keepdims=True)
    return yc * lax.rsqrt(var + LN_EPS) * w + b


def _row_mod(mod, idx, row0, nrows, n_ctx):
    rows = row0 + lax.broadcasted_iota(jnp.int32, (nrows, 1), 0)
    return jnp.where(rows < n_ctx, mod[idx:idx + 1, :], mod[6 + idx:7 + idx, :])


def _mod_kernel(c_ref, w_ref, b_ref, o_ref):
    s = _silu(c_ref[...])
    o_ref[0] = _dot3(s, w_ref[0]) + b_ref[0]


def _modulation(cvec, w_ada, b_ada):
    depth, d, n = w_ada.shape
    tn = n // 4
    rows = cvec.shape[0]
    return pl.pallas_call(
        _mod_kernel,
        grid=(depth, n // tn),
        in_specs=[pl.BlockSpec((rows, d), lambda l, j: (0, 0)),
                  pl.BlockSpec((1, d, tn), lambda l, j: (l, 0, j)),
                  pl.BlockSpec((1, 1, tn), lambda l, j: (l, 0, j))],
        out_specs=pl.BlockSpec((1, rows, tn), lambda l, j: (l, 0, j)),
        out_shape=jax.ShapeDtypeStruct((depth, rows, n), F32),
        compiler_params=_cparams(("arbitrary", "arbitrary")),
    )(cvec, w_ada, b_ada.reshape(depth, 1, n))


_W_K, _W_V, _W_KVA, _W_Q, _W_R, _W_CONV, _W_QA, _W_GATES, _W_GLR = 256, 512, 256, 256, 512, 1536, 512, 3072, 128
_OFF = np.cumsum([0, _W_K, _W_V, _W_KVA, _W_Q, _W_R, _W_CONV, _W_QA, _W_GATES, _W_GLR]).tolist()


def _wlayout_kernel(w_ref, o_ref):
    glr0 = GLA_KW + GLA_VW
    glr1 = glr0 + 2 * GATE_RANK
    n_in = w_ref.shape[2]
    o_ref[0, :, 0:glr0] = w_ref[0, :, 0:glr0].astype(ACT)
    o_ref[0, :, glr0:_OFF[-2]] = w_ref[0, :, glr1:n_in].astype(ACT)
    lane = lax.broadcasted_iota(jnp.int32, (1, _W_GLR), 1)
    tail = jnp.where(lane < 2 * GATE_RANK, w_ref[0, :, glr0:glr0 + _W_GLR], 0.0)
    o_ref[0, :, _OFF[-2]:_OFF[-1]] = tail.astype(ACT)


def _inproj_weight_layout(w_in):
    depth, d, n_in = w_in.shape
    rows = _largest_divisor(d, 256, 16)
    return pl.pallas_call(
        _wlayout_kernel,
        grid=(depth, d // rows),
        in_specs=[pl.BlockSpec((1, rows, n_in), lambda l, i: (l, i, 0))],
        out_specs=pl.BlockSpec((1, rows, _OFF[-1]), lambda l, i: (l, i, 0)),
        out_shape=jax.ShapeDtypeStruct((depth, d, _OFF[-1]), ACT),
        compiler_params=_cparams(("arbitrary", "arbitrary")),
    )(w_in)


def _inproj_kernel(n_ctx, bm, nsub, from_parts, *refs):
    if from_parts:
        npieces = bm // n_ctx
        ctx_ref, piece_refs, refs = refs[0], refs[1:1 + npieces], refs[1 + npieces:]
        x_ref = refs[-1]
        refs = refs[:-1]
        for s, p_ref in enumerate(piece_refs):
            rows_g = pl.program_id(1) * bm + s * n_ctx + lax.broadcasted_iota(jnp.int32, (n_ctx, 1), 0)
            x_ref[0, s * n_ctx:(s + 1) * n_ctx, :] = jnp.where(rows_g < n_ctx, ctx_ref[0], p_ref[0])
    else:
        x_ref, refs = refs[0], refs[1:]
    (mod_ref, w_ref, wg_ref, bg_ref,
     k_ref, v_ref, kva_ref, q_ref, r_ref, conv_ref, qa_ref, gates_ref, g_ref) = refs
    mod = mod_ref[0]
    sub_rows = bm // nsub
    for sub in range(nsub):
        rows = slice(sub * sub_rows, (sub + 1) * sub_rows)
        row0 = pl.program_id(1) * bm + sub * sub_rows
        sh = _row_mod(mod, 0, row0, sub_rows, n_ctx)
        sc = _row_mod(mod, 1, row0, sub_rows, n_ctx)
        h = (x_ref[0, rows, :] * (1.0 + sc) + sh).astype(ACT)

        def proj(c0, width):
            return _dot(h, w_ref[:, c0:c0 + width])

        z = _dot3(proj(_OFF[8], _W_GLR), wg_ref[...]) + bg_ref[...]
        g_ref[0, rows, :] = _log_sigmoid(z) * (1.0 / GATE_TAU)
        k_ref[0, rows, :] = proj(_OFF[0], _W_K).astype(ACT)
        v_ref[0, rows, :] = proj(_OFF[1], _W_V).astype(ACT)
        kva_ref[0, rows, :] = proj(_OFF[2], _W_KVA).astype(ACT)
        q_ref[0, rows, :] = (proj(_OFF[3], _W_Q) * (GLA_DK ** -0.5)).astype(ACT)
        r_ref[0, rows, :] = proj(_OFF[4], _W_R).astype(ACT)
        for i in range(_W_CONV // 512):
            conv_ref[0, rows, i * 512:(i + 1) * 512] = proj(_OFF[5] + i * 512, 512).astype(ACT)
        qa_ref[0, rows, :] = (proj(_OFF[6], _W_QA) * (HEAD_DIM ** -0.5)).astype(ACT)
        for i in range(_W_GATES // 1024):
            gates_ref[0, rows, i * 1024:(i + 1) * 1024] = proj(_OFF[7] + i * 1024, 1024).astype(ACT)


def _inproj(xs, modb, w, wg, bg, n_ctx, bm, nsub, parts=None):
    if parts is None:
        bsz, t, d = xs.shape
    else:
        ctx, x = parts
        bsz, seq, d = x.shape
        t = n_ctx + seq
        assert bm % n_ctx == 0
    row = lambda width: pl.BlockSpec((1, bm, width), lambda b, r: (b, r, 0))
    out = lambda width, dt: jax.ShapeDtypeStruct((bsz, t, width), dt)
    resident = lambda a: pl.BlockSpec(a.shape, lambda b, r: (0,) * a.ndim, pipeline_mode=pl.Buffered(1))
    if parts is None:
        x_specs, x_args = [row(d)], [xs]
    else:
        npieces = bm // n_ctx
        piece = lambda s: pl.BlockSpec((1, n_ctx, d), lambda b, r: (b, jnp.maximum(r * npieces + s - 1, 0), 0))
        x_specs = [pl.BlockSpec((1, n_ctx, d), lambda b, r: (b, 0, 0))] + [piece(s) for s in range(npieces)]
        x_args = [ctx] + [x] * npieces
    out_specs = [row(_W_K), row(_W_V), row(_W_KVA), row(_W_Q), row(_W_R), row(_W_CONV), row(_W_QA),
                 row(_W_GATES), row(2 * GLA_KW)]
    out_shape = [out(_W_K, ACT), out(_W_V, ACT), out(_W_KVA, ACT), out(_W_Q, ACT), out(_W_R, ACT),
                 out(_W_CONV, ACT), out(_W_QA, ACT), out(_W_GATES, ACT), out(2 * GLA_KW, F32)]
    if parts is not None:
        out_specs.append(row(d))
        out_shape.append(out(d, F32))
    return pl.pallas_call(
        functools.partial(_inproj_kernel, n_ctx, bm, nsub, parts is not None),
        grid=(bsz, t // bm),
        in_specs=x_specs + [pl.BlockSpec((1, 16, d), lambda b, r: (b, 0, 0)),
                            resident(w), resident(wg), resident(bg)],
        out_specs=out_specs,
        out_shape=out_shape,
        compiler_params=_cparams(("arbitrary", "arbitrary")),
    )(*x_args, modb, w, wg, bg)


def _gla_kernel(n_ctx, q_ref, k_ref, v_ref, r_ref, g_ref, nw_ref, o_ref, b_scr, u_scr, dec_scr):
    t = q_ref.shape[1]
    L = GLA_CHUNK
    nch = t // L
    nch_ctx = n_ctx // L
    kw = GLA_KW

    ri = lax.broadcasted_iota(jnp.int32, (L, L), 0)
    ci = lax.broadcasted_iota(jnp.int32, (L, L), 1)
    tri_f = (ci <= ri).astype(ACT)
    tri_b = (ci >= ri).astype(ACT)
    lane_k = lax.broadcasted_iota(jnp.int32, (1, kw), 1)
    head_masks = [(lane_k >= h * GLA_DK) & (lane_k < (h + 1) * GLA_DK) for h in range(GLA_HEADS)]

    def rows(c):
        return pl.ds(pl.multiple_of(c * L, L), L)

    def pass1(c, carry):
        rs = rows(c)
        g = g_ref[0, rs, :]
        g_hi, g_lo = _split_hi_lo(g)
        b_f = _dot(tri_f, g_hi[:, :kw]) + _dot(tri_f, g_lo[:, :kw])
        b_b = _dot(tri_b, g_hi[:, kw:]) + _dot(tri_b, g_lo[:, kw:])
        b_scr[rs, :kw] = b_f
        b_scr[rs, kw:] = b_b
        tot_f = b_f[L - 1:L, :]
        tot_b = b_b[0:1, :]
        k = k_ref[0, rs, :].astype(F32)
        ku_f = (k * jnp.exp(tot_f - b_f)).astype(ACT)
        ku_b = (k * jnp.exp(tot_b - b_b)).astype(ACT)
        ku = jnp.concatenate([ku_f, ku_b], axis=1)
        vt = v_ref[0, rs, :].astype(F32).T.astype(ACT)
        u_full = _dot(vt, ku)
        u_f = jnp.zeros((GLA_DV, kw), F32)
        u_b = jnp.zeros((GLA_DV, kw), F32)
        for h in range(GLA_HEADS):
            blk = u_full[h * GLA_DV:(h + 1) * GLA_DV, :]
            u_f = u_f + jnp.where(head_masks[h], blk[:, :kw], 0.0)
            u_b = u_b + jnp.where(head_masks[h], blk[:, kw:], 0.0)
        u_scr[c, 0] = u_f
        u_scr[c, 1] = u_b
        dec_scr[c, 0:1, :] = jnp.exp(tot_f)
        dec_scr[c, 1:2, :] = jnp.exp(tot_b)
        return carry

    unroll = 6 if nch % 6 == 0 else 1
    lax.fori_loop(0, nch, pass1, 0, unroll=unroll)

    def scan_f(c, s):
        u = u_scr[c, 0]
        u_scr[c, 0] = s
        return s * dec_scr[c, 0:1, :] + u

    lax.fori_loop(0, nch, scan_f, jnp.zeros((GLA_DV, kw), F32))

    def scan_b(i, s):
        c = jnp.where(i < nch_ctx, nch_ctx - 1 - i, nch - 1 - (i - nch_ctx))
        u = u_scr[c, 1]
        u_scr[c, 1] = s
        return s * dec_scr[c, 1:2, :] + u

    lax.fori_loop(0, nch, scan_b, jnp.zeros((GLA_DV, kw), F32))

    ri_st = lax.broadcasted_iota(jnp.int32, (GLA_HEADS * L, L), 0) & (L - 1)
    ci_st = lax.broadcasted_iota(jnp.int32, (GLA_HEADS * L, L), 1)
    tril = ci_st <= ri_st
    triu = ci_st >= ri_st
    nw = nw_ref[...]

    def pass2(c, carry):
        rs = rows(c)
        q = q_ref[0, rs, :].astype(F32)
        k = k_ref[0, rs, :].astype(F32)
        v = v_ref[0, rs, :]
        b_f = b_scr[rs, :kw]
        b_b = b_scr[rs, kw:]
        e_f = jnp.exp(b_f)
        e_b = jnp.exp(b_b)
        qd_f = q * e_f
        qd_b = q * e_b
        ki_f = (k / e_f).astype(ACT)
        ki_b = (k / e_b).astype(ACT)
        qs_f = jnp.concatenate([jnp.where(m, qd_f, 0.0) for m in head_masks], axis=0).astype(ACT)
        qs_b = jnp.concatenate([jnp.where(m, qd_b, 0.0) for m in head_masks], axis=0).astype(ACT)
        p = (jnp.where(tril, _dot_nt(qs_f, ki_f), 0.0)
             + jnp.where(triu, _dot_nt(qs_b, ki_b), 0.0)).astype(ACT)
        s_f = u_scr[c, 0].astype(ACT)
        s_b = u_scr[c, 1].astype(ACT)
        inter = _dot_nt(qs_f, s_f) + _dot_nt(qs_b, s_b)
        outs = []
        for h in range(GLA_HEADS):
            o_h = inter[h * L:(h + 1) * L, :] + _dot(p[h * L:(h + 1) * L, :], v[:, h * GLA_DV:(h + 1) * GLA_DV])
            ms = jnp.mean(o_h * o_h, axis=-1, keepdims=True)
            outs.append(o_h * lax.rsqrt(ms + RMS_EPS) * nw)
        o = jnp.concatenate(outs, axis=1)
        o_ref[0, rs, :] = (o * _silu(r_ref[0, rs, :].astype(F32))).astype(o_ref.dtype)
        return carry

    lax.fori_loop(0, nch, pass2, 0, unroll=unroll)


def _gla(q, k, v, r, g, nw, n_ctx):
    bsz, t, _ = q.shape
    nch = t // GLA_CHUNK
    full = lambda width: pl.BlockSpec((1, t, width), lambda b: (b, 0, 0))
    return pl.pallas_call(
        functools.partial(_gla_kernel, n_ctx),
        grid=(bsz,),
        in_specs=[full(GLA_KW), full(GLA_KW), full(GLA_VW), full(GLA_VW), full(2 * GLA_KW),
                  pl.BlockSpec((1, GLA_DV), lambda b: (0, 0))],
        out_specs=full(GLA_VW),
        out_shape=jax.ShapeDtypeStruct((bsz, t, GLA_VW), ACT),
        scratch_shapes=[pltpu.VMEM((t, 2 * GLA_KW), F32),
                        pltpu.VMEM((nch, 2, GLA_DV, GLA_KW), F32),
                        pltpu.VMEM((nch, 8, GLA_KW), F32)],
        compiler_params=_cparams(("arbitrary",)),
    )(q, k, v, r, g, nw)


def _conv_kernel(n_ctx, h_ref, gb_ref, gc_ref, w_ref, o_ref):
    t = h_ref.shape[1]
    a = gc_ref[0].astype(F32) * h_ref[0].astype(F32)
    rows = lax.broadcasted_iota(jnp.int32, (t, 1), 0)
    prev = jnp.where((rows == 0) | (rows == n_ctx), 0.0, pltpu.roll(a, 1, 0))
    nxt = jnp.where((rows == n_ctx - 1) | (rows == t - 1), 0.0, pltpu.roll(a, t - 1, 0))
    w = w_ref[...]
    y = w[0:1, :] * prev + w[1:2, :] * a + w[2:3, :] * nxt
    o_ref[0] = (gb_ref[0].astype(F32) * y).astype(o_ref.dtype)


def _short_conv(conv3, w, n_ctx):
    bsz, t, _ = conv3.shape
    spec = lambda i: pl.BlockSpec((1, t, CONV_W), lambda b: (b, 0, i))
    return pl.pallas_call(
        functools.partial(_conv_kernel, n_ctx),
        grid=(bsz,),
        in_specs=[spec(0), spec(1), spec(2), pl.BlockSpec(w.shape, lambda b: (0, 0))],
        out_specs=pl.BlockSpec((1, t, CONV_W), lambda b: (b, 0, 0)),
        out_shape=jax.ShapeDtypeStruct((bsz, t, CONV_W), ACT),
        compiler_params=_cparams(("arbitrary",)),
    )(conv3, conv3, conv3, w)


def _rope(x, cos, sin_signed):
    lane = lax.broadcasted_iota(jnp.int32, (1, LANES), 1)
    quarter = HEAD_DIM // 4
    swapped = jnp.where((lane % (2 * quarter)) < quarter,
                        pltpu.roll(x, LANES - quarter, 1), pltpu.roll(x, quarter, 1))
    return x * cos + swapped * sin_signed


def _attn_kernel(n_ctx, nb, sink_ref, q_ref, kv_ref, cosq_ref, sinq_ref, cos_all_ref, sin_all_ref, o_ref,
                 kd_scr, vd_scr, s_scr, p_scr):
    step = pl.program_id(1)

    @pl.when(step == 0)
    def _():
        lane = lax.broadcasted_iota(jnp.int32, (1, LANES), 1)
        first_half = lane < HEAD_DIM
        k = _rope(kv_ref[0, :, :LANES].astype(F32), cos_all_ref[...], sin_all_ref[...])
        v = kv_ref[0, :, LANES:].astype(F32)
        for src, dst in ((k, kd_scr), (v, vd_scr)):
            swapped = pltpu.roll(src, HEAD_DIM, 1)
            dst[0] = jnp.where(first_half, src, swapped).astype(ACT)
            dst[1] = jnp.where(first_half, swapped, src).astype(ACT)

    for sub in range(nb):
        _attn_block(n_ctx, step * nb + sub, sub, sink_ref, q_ref, cosq_ref, sinq_ref, o_ref, kd_scr, vd_scr,
                    s_scr.at[sub], p_scr.at[sub])


def _attn_block(n_ctx, j, sub, sink_ref, q_ref, cosq_ref, sinq_ref, o_ref, kd_scr, vd_scr, s_scr, p_scr):
    t = kd_scr.shape[1]
    blk = ATT_BLOCK
    n_loc = 3 * blk
    n_lat = t - n_ctx
    qrows = slice(sub * blk, (sub + 1) * blk)

    lane = lax.broadcasted_iota(jnp.int32, (1, LANES), 1)
    first_half = lane < HEAD_DIM
    cos = cosq_ref[qrows, :]
    sin = sinq_ref[qrows, :]
    qs = []
    for gi in range(ATT_HEADS // 2):
        qg = _rope(q_ref[0, qrows, gi * LANES:(gi + 1) * LANES].astype(F32), cos, sin)
        qs.append(jnp.where(first_half, qg, 0.0))
        qs.append(jnp.where(first_half, 0.0, qg))
    qst = jnp.concatenate(qs, axis=0).astype(ACT)

    jl = j - n_ctx // blk
    start = jnp.clip((jl - 1) * blk, 0, n_lat - n_loc)
    loc = pl.ds(pl.multiple_of(n_ctx + start, blk), n_loc)
    hrows = ATT_GROUP * blk
    for kv in range(ATT_KV_HEADS):
        rs = slice(kv * hrows, (kv + 1) * hrows)
        s_scr[rs, 0:n_ctx] = _dot_nt(qst[rs, :], kd_scr[kv, 0:n_ctx, :])
        s_scr[rs, n_ctx:] = _dot_nt(qst[rs, :], kd_scr[kv, loc, :])
    q_base = jnp.where(jl >= 0, jl * blk, -(t + WINDOW + blk))
    qpos = q_base + lax.broadcasted_iota(jnp.int32, (blk, n_loc), 0)
    kpos = start + lax.broadcasted_iota(jnp.int32, (blk, n_loc), 1)
    bias = jnp.where(jnp.abs(qpos - kpos) <= WINDOW, 0.0, NEG_INF)
    ones = jnp.ones((n_ctx + n_loc, LANES), ACT)

    sink_exp = []
    for i in range(ATT_HEADS):
        rows = slice(i * blk, (i + 1) * blk)
        sink = sink_ref[i] * LOG2E
        sc = s_scr[rows, 0:n_ctx]
        sl = s_scr[rows, n_ctx:] + bias
        m = jnp.maximum(jnp.maximum(jnp.max(sc, axis=-1, keepdims=True),
                                    jnp.max(sl, axis=-1, keepdims=True)), sink)
        p_scr[rows, 0:n_ctx] = jnp.exp2(sc - m).astype(ACT)
        p_scr[rows, n_ctx:] = jnp.exp2(sl - m).astype(ACT)
        sink_exp.append(jnp.exp2(sink - m))
    outs = []
    for kv in range(ATT_KV_HEADS):
        rs = slice(kv * hrows, (kv + 1) * hrows)
        v_aug = jnp.concatenate(
            [jnp.concatenate([vd_scr[kv, 0:n_ctx, :], vd_scr[kv, loc, :]], axis=0), ones], axis=1)
        o_kv = _dot(p_scr[rs, :], v_aug)
        for g in range(ATT_GROUP):
            i = kv * ATT_GROUP + g
            rows = slice(g * blk, (g + 1) * blk)
            outs.append(o_kv[rows, :LANES] / (o_kv[rows, LANES:] + sink_exp[i]))
    for gi in range(ATT_HEADS // 2):
        o_ref[0, qrows, gi * LANES:(gi + 1) * LANES] = jnp.where(
            first_half, outs[2 * gi], outs[2 * gi + 1]).astype(o_ref.dtype)


def _attention(qa, kva, sink, cos, sin_signed, n_ctx, nb):
    bsz, t, _ = qa.shape
    blk = ATT_BLOCK
    rows = nb * blk
    return pl.pallas_call(
        functools.partial(_attn_kernel, n_ctx, nb),
        grid=(bsz, t // rows),
        in_specs=[pl.BlockSpec(memory_space=pltpu.SMEM),
                  pl.BlockSpec((1, rows, ATT_QW), lambda b, j: (b, j, 0)),
                  pl.BlockSpec((1, t, 2 * ATT_KVW), lambda b, j: (b, 0, 0)),
                  pl.BlockSpec((rows, LANES), lambda b, j: (j, 0)),
                  pl.BlockSpec((rows, LANES), lambda b, j: (j, 0)),
                  pl.BlockSpec((t, LANES), lambda b, j: (0, 0)),
                  pl.BlockSpec((t, LANES), lambda b, j: (0, 0))],
        out_specs=pl.BlockSpec((1, rows, ATT_QW), lambda b, j: (b, j, 0)),
        out_shape=jax.ShapeDtypeStruct((bsz, t, ATT_QW), ACT),
        scratch_shapes=[pltpu.VMEM((ATT_KV_HEADS, t, LANES), ACT),
                        pltpu.VMEM((ATT_KV_HEADS, t, LANES), ACT),
                        pltpu.VMEM((nb, ATT_HEADS * blk, n_ctx + 3 * blk), F32),
                        pltpu.VMEM((nb, ATT_HEADS * blk, n_ctx + 3 * blk), ACT)],
        compiler_params=_cparams(("arbitrary", "arbitrary")),
    )(sink, qa, kva, cos * LOG2E, sin_signed * LOG2E, cos, sin_signed)


def _merge_kernel(n_ctx, bm, nsub, ya_ref, yb_ref, yc_ref, ma_ref, mb_ref, mc_ref, x_ref, mod_ref,
                  wa_ref, wb_ref, wc_ref, wo_ref, lnw_ref, lnb_ref, o_ref):
    sub_rows = bm // nsub
    for sub in range(nsub):
        rows = slice(sub * sub_rows, (sub + 1) * sub_rows)
        row0 = pl.program_id(1) * bm + sub * sub_rows
        m = (_sigmoid(ma_ref[0, rows, :].astype(F32)) * _dot(ya_ref[0, rows, :], wa_ref[...])
             + _sigmoid(mb_ref[0, rows, :].astype(F32)) * _dot(yb_ref[0, rows, :], wb_ref[...])
             + _sigmoid(mc_ref[0, rows, :].astype(F32)) * _dot(yc_ref[0, rows, :], wc_ref[...]))
        mix = _dot(m.astype(ACT), wo_ref[...])
        gate = _row_mod(mod_ref[0], 2, row0, sub_rows, n_ctx)
        o_ref[0, rows, :] = _layer_norm(DEEPNORM_ALPHA * x_ref[0, rows, :] + gate * mix,
                                        lnw_ref[...], lnb_ref[...])


def _merge(ya, yb, yc, gates, xs, modb, wa, wb, wc, wo, lnw, lnb, n_ctx, bm, nsub):
    bsz, t, d = xs.shape
    row = lambda width, i=0: pl.BlockSpec((1, bm, width), lambda b, r: (b, r, i))
    const = lambda a: pl.BlockSpec(a.shape, lambda b, r: (0,) * a.ndim, pipeline_mode=pl.Buffered(1))
    return pl.pallas_call(
        functools.partial(_merge_kernel, n_ctx, bm, nsub),
        grid=(bsz, t // bm),
        in_specs=[row(GLA_VW), row(CONV_W), row(ATT_QW), row(d, 0), row(d, 1), row(d, 2), row(d),
                  pl.BlockSpec((1, 16, d), lambda b, r: (b, 0, 0)),
                  const(wa), const(wb), const(wc), const(wo), const(lnw), const(lnb)],
        out_specs=row(d),
        out_shape=jax.ShapeDtypeStruct((bsz, t, d), F32),
        compiler_params=_cparams(("arbitrary", "arbitrary")),
    )(ya, yb, yc, gates, gates, gates, xs, modb, wa, wb, wc, wo, lnw, lnb)


def _ffn_kernel(n_ctx, rb, tn, t, row_off, nx, *refs):
    x_refs = refs[:nx]
    (xp_ref, xn_ref, mod_ref, wup_ref, cw_ref, wd_ref, lnw_ref, lnb_ref, o_ref, h_scr, act_scr) = refs[nx:]
    piece = rb // nx
    d_ff = wd_ref.shape[0]
    r = pl.program_id(1)
    row0 = row_off + r * rb
    ext = rb + 2 * HALO
    edge = xp_ref.shape[1]
    mod = mod_ref[0]

    def modulated(xv, first_row):
        n = xv.shape[0]
        return xv * (1.0 + _row_mod(mod, 4, first_row, n, n_ctx)) + _row_mod(mod, 3, first_row, n, n_ctx)

    def same_sequence(rows, neighbour):
        in_lat = (rows >= n_ctx).astype(jnp.int32)
        return (rows >= 0) & (rows < t) & (in_lat == (neighbour >= n_ctx).astype(jnp.int32))

    rows_p = row0 - edge + lax.broadcasted_iota(jnp.int32, (edge, 1), 0)
    rows_n = row0 + rb + lax.broadcasted_iota(jnp.int32, (edge, 1), 0)
    hp = jnp.where(same_sequence(rows_p, row0), modulated(xp_ref[0], row0 - edge), 0.0)
    hn = jnp.where(same_sequence(rows_n, row0 + rb - 1), modulated(xn_ref[0], row0 + rb), 0.0)
    zpad = jnp.zeros((HALO - edge, xp_ref.shape[2]), F32)
    h_scr[0:HALO, :] = jnp.concatenate([zpad, hp], axis=0).astype(ACT)
    for s, x_ref in enumerate(x_refs):
        h_scr[HALO + s * piece:HALO + (s + 1) * piece, :] = modulated(x_ref[0], row0 + s * piece).astype(ACT)
    h_scr[HALO + rb:ext, :] = jnp.concatenate([hn, zpad], axis=0).astype(ACT)

    def conv3(u, w, keep_prev=None, keep_next=None):
        prev = pltpu.roll(u, 1, 0)
        nxt = pltpu.roll(u, u.shape[0] - 1, 0)
        if keep_prev is not None:
            prev = jnp.where(keep_prev, prev, 0.0)
            nxt = jnp.where(keep_next, nxt, 0.0)
        return w[0:1, :] * prev + w[1:2, :] * u + w[2:3, :] * nxt

    lb = (n_ctx - row_off) % rb if n_ctx > row_off else 0
    fix = HALO + 8
    he = h_scr[...]
    nchunks = d_ff // tn

    def up(n):
        return (_dot(he, wup_ref[:, n * tn:(n + 1) * tn]),
                _dot(he, wup_ref[:, d_ff + n * tn:d_ff + (n + 1) * tn]))

    for n in range(nchunks):
        cg = slice(n * tn, (n + 1) * tn)
        cv = slice(d_ff + n * tn, d_ff + (n + 1) * tn)
        ug, uv = up(n)
        wg = cw_ref[:, cg]
        wv = cw_ref[:, cv]
        gate = conv3(ug, wg)[HALO:HALO + rb, :]
        val = conv3(uv, wv)[HALO:HALO + rb, :]
        act_scr[:, cg] = (_silu(gate) * val).astype(ACT)
        if lb:
            rows_g = row0 + lb - fix + lax.broadcasted_iota(jnp.int32, (2 * fix, 1), 0)
            win = slice(HALO + lb - fix, HALO + lb + fix)
            g_w = conv3(ug[win, :], wg, rows_g != n_ctx, rows_g != n_ctx - 1)
            v_w = conv3(uv[win, :], wv, rows_g != n_ctx, rows_g != n_ctx - 1)
            a_w = _silu(g_w) * v_w
            act_scr[lb - HALO:lb + HALO, cg] = a_w[fix - HALO:fix + HALO, :].astype(ACT)

    y = _dot(act_scr[...], wd_ref[...])
    for s, x_ref in enumerate(x_refs):
        rows = slice(s * piece, (s + 1) * piece)
        gate2 = _row_mod(mod, 5, row0 + s * piece, piece, n_ctx)
        o_ref[0, rows, :] = _layer_norm(DEEPNORM_ALPHA * x_ref[0] + gate2 * y[rows, :], lnw_ref[...], lnb_ref[...])


def _ffn(xs, modb, w_up, w_conv, w_down, lnw, lnb, n_ctx, rb, tn, row_off=0):
    bsz, t, d = xs.shape
    d_ff = w_down.shape[0]
    edge = 8
    rows_out = t - row_off
    assert rows_out % rb == 0 and row_off % edge == 0
    lb = (n_ctx - row_off) % rb if n_ctx > row_off else 0
    assert lb == 0 or (2 * HALO <= lb <= rb - 2 * HALO and lb % HALO == 0)
    piece = int(np.gcd(rb, row_off)) if row_off else rb
    nx = rb // piece
    nb = rb // edge
    first = row_off // edge
    resident = lambda a: pl.BlockSpec(a.shape, lambda b, r: (0,) * a.ndim, pipeline_mode=pl.Buffered(1))
    x_piece = lambda s: pl.BlockSpec((1, piece, d), lambda b, r: (b, r * nx + s + row_off // piece, 0))
    return pl.pallas_call(
        functools.partial(_ffn_kernel, n_ctx, rb, tn, t, row_off, nx),
        grid=(bsz, rows_out // rb),
        in_specs=[x_piece(s) for s in range(nx)] + [
            pl.BlockSpec((1, edge, d), lambda b, r: (b, jnp.maximum(first + r * nb - 1, 0), 0)),
            pl.BlockSpec((1, edge, d), lambda b, r: (b, jnp.minimum(first + (r + 1) * nb, t // edge - 1), 0)),
            pl.BlockSpec((1, 16, d), lambda b, r: (b, 0, 0)),
            resident(w_up), resident(w_conv), resident(w_down), resident(lnw), resident(lnb)],
        out_specs=pl.BlockSpec((1, rb, d), lambda b, r: (b, r, 0)),
        out_shape=jax.ShapeDtypeStruct((bsz, rows_out, d), F32),
        scratch_shapes=[pltpu.VMEM((rb + 2 * HALO, d), ACT), pltpu.VMEM((rb, d_ff), ACT)],
        compiler_params=_cparams(("arbitrary", "arbitrary")),
    )(*([xs] * (nx + 2)), modb, w_up, w_conv, w_down, lnw, lnb)


def _rope_tables(n_ctx, seq):
    rows = seq // GRID_W
    row_idx = np.repeat(np.arange(rows), GRID_W).astype(np.float32)
    col_idx = np.tile(np.arange(GRID_W), rows).astype(np.float32)
    half = HEAD_DIM // 2
    inv_freq = (ROPE_THETA ** (-np.arange(0, half, 2, dtype=np.float32) / half)).astype(np.float32)
    ang_r = jnp.asarray(row_idx[:, None] * inv_freq)
    ang_c = jnp.asarray(col_idx[:, None] * inv_freq)
    cos = jnp.concatenate([jnp.cos(ang_r)] * 2 + [jnp.cos(ang_c)] * 2, axis=-1)
    sin = jnp.concatenate([-jnp.sin(ang_r), jnp.sin(ang_r), -jnp.sin(ang_c), jnp.sin(ang_c)], axis=-1)
    cos = jnp.concatenate([jnp.ones((n_ctx, HEAD_DIM), F32), cos], axis=0)
    sin = jnp.concatenate([jnp.zeros((n_ctx, HEAD_DIM), F32), sin], axis=0)
    return jnp.tile(cos, (1, 2)), jnp.tile(sin, (1, 2))


def _largest_divisor(n, cap, multiple):
    best = multiple
    for cand in range(multiple, cap + 1, multiple):
        if n % cand == 0:
            best = cand
    return best


def kernel(x, c, ctx, c_ctx, w_ada, b_ada, w_in, gla_gate_up_f, gla_gate_bias_f, gla_gate_up_b,
           gla_gate_bias_b, gla_norm_w, conv_w, att_sink, w_branch_a, w_branch_b, w_branch_c, w_out,
           ln1_w, ln1_b, ffn_up, ffn_conv, ffn_down, ln2_w, ln2_b):
    bsz, seq, d = x.shape
    n_ctx = ctx.shape[1]
    depth = w_in.shape[0]
    t = n_ctx + seq
    assert n_ctx % GLA_CHUNK == 0 and seq % GLA_CHUNK == 0 and seq >= 3 * ATT_BLOCK
    rb = _largest_divisor(t, 768, 128)
    rb_last = _largest_divisor(seq, 768, 128)
    bm = rb
    nsub = 2 if bm % 256 == 0 else 1
    d_ff = ffn_down.shape[1]
    tn = 256
    nb_att = 2 if (t // ATT_BLOCK) % 2 == 0 else 1

    cvec = jnp.zeros((16, d), F32).at[:bsz].set(c).at[bsz].set(c_ctx)
    mod = _modulation(cvec, w_ada, b_ada)
    mod_lat = mod[:, :bsz].reshape(depth, bsz, 6, d)
    mod_ctx = jnp.broadcast_to(mod[:, bsz].reshape(depth, 1, 6, d), (depth, bsz, 6, d))
    modb = jnp.concatenate([mod_ctx, mod_lat, jnp.zeros((depth, bsz, 4, d), F32)], axis=2)

    assert w_in.shape[2] - 2 * GATE_RANK == _OFF[-2]
    w_in_r = _inproj_weight_layout(w_in)
    zpad = jnp.zeros((depth, GATE_RANK, GLA_KW), F32)
    w_gate = jnp.concatenate(
        [jnp.concatenate([gla_gate_up_f, zpad], axis=2), jnp.concatenate([zpad, gla_gate_up_b], axis=2),
         jnp.zeros((depth, _W_GLR - 2 * GATE_RANK, 2 * GLA_KW), F32)], axis=1)
    b_gate = jnp.concatenate([gla_gate_bias_f, gla_gate_bias_b], axis=1).reshape(depth, 1, 2 * GLA_KW)

    cos, sin_signed = _rope_tables(n_ctx, seq)
    nw = gla_norm_w.reshape(depth, 1, GLA_DV)
    wa = w_branch_a.astype(ACT)
    wb = w_branch_b.astype(ACT)
    wc = w_branch_c.astype(ACT)
    wo = w_out.astype(ACT)
    w_up = ffn_up.astype(ACT)
    w_dn = ffn_down.astype(ACT)

    xs = None
    for l in range(depth):
        last = l == depth - 1
        if l == 0 and bm % n_ctx == 0:
            *proj, xs = _inproj(None, modb[l], w_in_r[l], w_gate[l], b_gate[l], n_ctx, bm, nsub, parts=(ctx, x))
        else:
            if xs is None:
                xs = jnp.concatenate([ctx, x], axis=1)
            proj = _inproj(xs, modb[l], w_in_r[l], w_gate[l], b_gate[l], n_ctx, bm, nsub)
        k, v, kva, q, r, conv3, qa, gates, g = proj
        ya = _gla(q, k, v, r, g, nw[l], n_ctx)
        yb = _short_conv(conv3, conv_w[l], n_ctx)
        yc = _attention(qa, kva, att_sink[l], cos, sin_signed, n_ctx, nb_att)
        xs = _merge(ya, yb, yc, gates, xs, modb[l], wa[l], wb[l], wc[l], wo[l],
                    ln1_w[l].reshape(1, d), ln1_b[l].reshape(1, d), n_ctx, bm, nsub)
        xs = _ffn(xs, modb[l], w_up[l], ffn_conv[l], w_dn[l], ln2_w[l].reshape(1, d), ln2_b[l].reshape(1, d),
                  n_ctx, rb_last if last else rb, tn, row_off=n_ctx if last else 0)
    return xs
```

```python
import functools

import numpy as np
import jax
import jax.numpy as jnp
from jax import lax
from jax.experimental import pallas as pl
from jax.experimental.pallas import tpu as pltpu

DEPTH_FOR_DEEPNORM = 4
DEEPNORM_ALPHA = (2 * DEPTH_FOR_DEEPNORM) ** 0.25
LN_EPS = 1e-5
RMS_EPS = 1e-6
GATE_TAU = 16.0
NEG_INF = -1e30
ROPE_THETA = 10000.0
LOG2E = 1.4426950408889634

GRID_W = 64
GLA_HEADS = 4
GLA_DK = 64
GLA_DV = 128
GLA_KW = GLA_HEADS * GLA_DK
GLA_VW = GLA_HEADS * GLA_DV
GATE_RANK = 16
CONV_W = 512
ATT_HEADS = 8
ATT_KV_HEADS = 2
ATT_GROUP = ATT_HEADS // ATT_KV_HEADS
HEAD_DIM = 64
ATT_QW = ATT_HEADS * HEAD_DIM
ATT_KVW = ATT_KV_HEADS * HEAD_DIM
WINDOW = 128
ATT_BLOCK = 128

LANES = 128
GLA_CHUNK = 128
HALO = 16
VMEM_LIMIT = 56 * 1024 * 1024

ACT = jnp.bfloat16
F32 = jnp.float32


def _cparams(sem):
    return pltpu.CompilerParams(dimension_semantics=sem, vmem_limit_bytes=VMEM_LIMIT)


def _split_hi_lo(a):
    hi = a.astype(ACT)
    lo = (a - hi.astype(F32)).astype(ACT)
    return hi, lo


def _dot(a, b):
    return jnp.dot(a, b, preferred_element_type=F32)


def _dot_nt(a, b):
    return lax.dot_general(a, b, (((1,), (1,)), ((), ())), preferred_element_type=F32)


def _dot3(a, b):
    ah, al = _split_hi_lo(a)
    bh, bl = _split_hi_lo(b)
    return _dot(ah, bh) + _dot(ah, bl) + _dot(al, bh)


def _sigmoid(x):
    return 1.0 / (1.0 + jnp.exp(-x))


def _silu(x):
    return x * _sigmoid(x)


def _log_sigmoid(x):
    return jnp.minimum(x, 0.0) - jnp.log(1.0 + jnp.exp(-jnp.abs(x)))


def _layer_norm(y, w, b):
    mu = jnp.mean(y, axis=-1, keepdims=True)
    yc = y - mu
    var = jnp.mean(yc * yc, axis=-1, keepdims=True)
    return yc * lax.rsqrt(var + LN_EPS) * w + b


def _row_mod(mod, idx, row0, nrows, n_ctx):
    rows = row0 + lax.broadcasted_iota(jnp.int32, (nrows, 1), 0)
    return jnp.where(rows < n_ctx, mod[idx:idx + 1, :], mod[6 + idx:7 + idx, :])


def _mod_kernel(c_ref, w_ref, b_ref, o_ref):
    s = _silu(c_ref[...])
    o_ref[0] = _dot3(s, w_ref[0]) + b_ref[0]


def _modulation(cvec, w_ada, b_ada):
    depth, d, n = w_ada.shape
    tn = n // 4
    rows = cvec.shape[0]
    return pl.pallas_call(
        _mod_kernel,
        grid=(depth, n // tn),
        in_specs=[pl.BlockSpec((rows, d), lambda l, j: (0, 0)),
                  pl.BlockSpec((1, d, tn), lambda l, j: (l, 0, j)),
                  pl.BlockSpec((1, 1, tn), lambda l, j: (l, 0, j))],
        out_specs=pl.BlockSpec((1, rows, tn), lambda l, j: (l, 0, j)),
        out_shape=jax.ShapeDtypeStruct((depth, rows, n), F32),
        compiler_params=_cparams(("arbitrary", "arbitrary")),
    )(cvec, w_ada, b_ada.reshape(depth, 1, n))


_W_K, _W_V, _W_GLR, _W_KVA, _W_Q, _W_R, _W_CONV, _W_QA, _W_GATES = (
    GLA_KW, GLA_VW, 2 * GATE_RANK, 2 * ATT_KVW, GLA_KW, GLA_VW, 3 * CONV_W, ATT_QW, 3 * 1024)
_OFF = np.cumsum([0, _W_K, _W_V, _W_GLR, _W_KVA, _W_Q, _W_R, _W_CONV, _W_QA, _W_GATES]).tolist()


def _layer_spec(a, layer):
    tail = (0,) * (a.ndim - 1)
    return pl.BlockSpec((1,) + a.shape[1:], lambda *_: (layer,) + tail, pipeline_mode=pl.Buffered(1))


def _mod_spec(modb, layer):
    return pl.BlockSpec((1, 1) + modb.shape[2:], lambda b, *_: (layer, b, 0, 0))


def _inproj_kernel(n_ctx, bm, nsub, from_parts, *refs):
    if from_parts:
        npieces = bm // n_ctx
        ctx_ref, piece_refs, refs = refs[0], refs[1:1 + npieces], refs[1 + npieces:]
        x_ref = refs[-1]
        refs = refs[:-1]
        for s, p_ref in enumerate(piece_refs):
            rows_g = pl.program_id(1) * bm + s * n_ctx + lax.broadcasted_iota(jnp.int32, (n_ctx, 1), 0)
            x_ref[0, s * n_ctx:(s + 1) * n_ctx, :] = jnp.where(rows_g < n_ctx, ctx_ref[0], p_ref[0])
    else:
        x_ref, refs = refs[0], refs[1:]
    (mod_ref, wt_ref, wg_ref, bg_ref,
     k_ref, v_ref, kva_ref, q_ref, r_ref, conv_ref, qa_ref, gates_ref, g_ref) = refs
    mod = mod_ref[0, 0]
    sub_rows = bm // nsub
    for sub in range(nsub):
        rows = slice(sub * sub_rows, (sub + 1) * sub_rows)
        row0 = pl.program_id(1) * bm + sub * sub_rows
        sh = _row_mod(mod, 0, row0, sub_rows, n_ctx)
        sc = _row_mod(mod, 1, row0, sub_rows, n_ctx)
        h = (x_ref[0, rows, :] * (1.0 + sc) + sh).astype(ACT)

        def proj(c0, width):
            return _dot_nt(h, wt_ref[0, c0:c0 + width, :])

        z = _dot3(proj(_OFF[2], _W_GLR), wg_ref[0]) + bg_ref[0]
        g_ref[0, rows, :] = _log_sigmoid(z) * (1.0 / GATE_TAU)
        k_ref[0, rows, :] = proj(_OFF[0], _W_K).astype(ACT)
        v_ref[0, rows, :] = proj(_OFF[1], _W_V).astype(ACT)
        kva_ref[0, rows, :] = proj(_OFF[3], _W_KVA).astype(ACT)
        q_ref[0, rows, :] = (proj(_OFF[4], _W_Q) * (GLA_DK ** -0.5)).astype(ACT)
        r_ref[0, rows, :] = proj(_OFF[5], _W_R).astype(ACT)
        for i in range(_W_CONV // 512):
            conv_ref[0, rows, i * 512:(i + 1) * 512] = proj(_OFF[6] + i * 512, 512).astype(ACT)
        qa_ref[0, rows, :] = (proj(_OFF[7], _W_QA) * (HEAD_DIM ** -0.5)).astype(ACT)
        for i in range(_W_GATES // 1024):
            gates_ref[0, rows, i * 1024:(i + 1) * 1024] = proj(_OFF[8] + i * 1024, 1024).astype(ACT)


def _inproj(xs, layer, modb, wt, wg, bg, n_ctx, bm, nsub, parts=None):
    if parts is None:
        bsz, t, d = xs.shape
    else:
        ctx, x = parts
        bsz, seq, d = x.shape
        t = n_ctx + seq
        assert bm % n_ctx == 0
    row = lambda width: pl.BlockSpec((1, bm, width), lambda b, r: (b, r, 0))
    out = lambda width, dt: jax.ShapeDtypeStruct((bsz, t, width), dt)
    if parts is None:
        x_specs, x_args = [row(d)], [xs]
    else:
        npieces = bm // n_ctx
        piece = lambda s: pl.BlockSpec((1, n_ctx, d), lambda b, r: (b, jnp.maximum(r * npieces + s - 1, 0), 0))
        x_specs = [pl.BlockSpec((1, n_ctx, d), lambda b, r: (b, 0, 0))] + [piece(s) for s in range(npieces)]
        x_args = [ctx] + [x] * npieces
    out_specs = [row(_W_K), row(_W_V), row(_W_KVA), row(_W_Q), row(_W_R), row(_W_CONV), row(_W_QA),
                 row(_W_GATES), row(2 * GLA_KW)]
    out_shape = [out(_W_K, ACT), out(_W_V, ACT), out(_W_KVA, ACT), out(_W_Q, ACT), out(_W_R, ACT),
                 out(_W_CONV, ACT), out(_W_QA, ACT), out(_W_GATES, ACT), out(2 * GLA_KW, F32)]
    if parts is not None:
        out_specs.append(row(d))
        out_shape.append(out(d, F32))
    return pl.pallas_call(
        functools.partial(_inproj_kernel, n_ctx, bm, nsub, parts is not None),
        grid=(bsz, t // bm),
        in_specs=x_specs + [_mod_spec(modb, layer),
                            _layer_spec(wt, layer), _layer_spec(wg, layer), _layer_spec(bg, layer)],
        out_specs=out_specs,
        out_shape=out_shape,
        compiler_params=_cparams(("arbitrary", "arbitrary")),
    )(*x_args, modb, wt, wg, bg)


def _gla_kernel(n_ctx, q_ref, k_ref, v_ref, r_ref, g_ref, nw_ref, o_ref, b_scr, u_scr, dec_scr):
    t = q_ref.shape[1]
    L = GLA_CHUNK
    nch = t // L
    nch_ctx = n_ctx // L
    kw = GLA_KW

    ri = lax.broadcasted_iota(jnp.int32, (L, L), 0)
    ci = lax.broadcasted_iota(jnp.int32, (L, L), 1)
    tri_f = (ci <= ri).astype(ACT)
    tri_b = (ci >= ri).astype(ACT)
    lane_k = lax.broadcasted_iota(jnp.int32, (1, kw), 1)
    head_masks = [(lane_k >= h * GLA_DK) & (lane_k < (h + 1) * GLA_DK) for h in range(GLA_HEADS)]

    def rows(c):
        return pl.ds(pl.multiple_of(c * L, L), L)

    def pass1(c, carry):
        rs = rows(c)
        g = g_ref[0, rs, :]
        g_hi, g_lo = _split_hi_lo(g)
        b_f = _dot(tri_f, g_hi[:, :kw]) + _dot(tri_f, g_lo[:, :kw])
        b_b = _dot(tri_b, g_hi[:, kw:]) + _dot(tri_b, g_lo[:, kw:])
        b_scr[rs, :kw] = b_f
        b_scr[rs, kw:] = b_b
        tot_f = b_f[L - 1:L, :]
        tot_b = b_b[0:1, :]
        k = k_ref[0, rs, :].astype(F32)
        ku_f = (k * jnp.exp(tot_f - b_f)).astype(ACT)
        ku_b = (k * jnp.exp(tot_b - b_b)).astype(ACT)
        ku = jnp.concatenate([ku_f, ku_b], axis=1)
        vt = v_ref[0, rs, :].astype(F32).T.astype(ACT)
        u_full = _dot(vt, ku)
        u_f = jnp.zeros((GLA_DV, kw), F32)
        u_b = jnp.zeros((GLA_DV, kw), F32)
        for h in range(GLA_HEADS):
            blk = u_full[h * GLA_DV:(h + 1) * GLA_DV, :]
            u_f = u_f + jnp.where(head_masks[h], blk[:, :kw], 0.0)
            u_b = u_b + jnp.where(head_masks[h], blk[:, kw:], 0.0)
        u_scr[c, 0] = u_f
        u_scr[c, 1] = u_b
        dec_scr[c, 0:1, :] = jnp.exp(tot_f)
        dec_scr[c, 1:2, :] = jnp.exp(tot_b)
        return carry

    unroll = 6 if nch % 6 == 0 else 1
    lax.fori_loop(0, nch, pass1, 0, unroll=unroll)

    def scan_f(c, s):
        u = u_scr[c, 0]
        u_scr[c, 0] = s
        return s * dec_scr[c, 0:1, :] + u

    lax.fori_loop(0, nch, scan_f, jnp.zeros((GLA_DV, kw), F32))

    def scan_b(i, s):
        c = jnp.where(i < nch_ctx, nch_ctx - 1 - i, nch - 1 - (i - nch_ctx))
        u = u_scr[c, 1]
        u_scr[c, 1] = s
        return s * dec_scr[c, 1:2, :] + u

    lax.fori_loop(0, nch, scan_b, jnp.zeros((GLA_DV, kw), F32))

    ri_st = lax.broadcasted_iota(jnp.int32, (GLA_HEADS * L, L), 0) & (L - 1)
    ci_st = lax.broadcasted_iota(jnp.int32, (GLA_HEADS * L, L), 1)
    tril = ci_st <= ri_st
    triu = ci_st >= ri_st
    nw = nw_ref[...]

    def pass2(c, carry):
        rs = rows(c)
        q = q_ref[0, rs, :].astype(F32)
        k = k_ref[0, rs, :].astype(F32)
        v = v_ref[0, rs, :]
        b_f = b_scr[rs, :kw]
        b_b = b_scr[rs, kw:]
        e_f = jnp.exp(b_f)
        e_b = jnp.exp(b_b)
        qd_f = q * e_f
        qd_b = q * e_b
        ki_f = (k / e_f).astype(ACT)
        ki_b = (k / e_b).astype(ACT)
        qs_f = jnp.concatenate([jnp.where(m, qd_f, 0.0) for m in head_masks], axis=0).astype(ACT)
        qs_b = jnp.concatenate([jnp.where(m, qd_b, 0.0) for m in head_masks], axis=0).astype(ACT)
        p = (jnp.where(tril, _dot_nt(qs_f, ki_f), 0.0)
             + jnp.where(triu, _dot_nt(qs_b, ki_b), 0.0)).astype(ACT)
        s_f = u_scr[c, 0].astype(ACT)
        s_b = u_scr[c, 1].astype(ACT)
        inter = _dot_nt(qs_f, s_f) + _dot_nt(qs_b, s_b)
        outs = []
        for h in range(GLA_HEADS):
            o_h = inter[h * L:(h + 1) * L, :] + _dot(p[h * L:(h + 1) * L, :], v[:, h * GLA_DV:(h + 1) * GLA_DV])
            ms = jnp.mean(o_h * o_h, axis=-1, keepdims=True)
            outs.append(o_h * lax.rsqrt(ms + RMS_EPS) * nw)
        o = jnp.concatenate(outs, axis=1)
        o_ref[0, rs, :] = (o * _silu(r_ref[0, rs, :].astype(F32))).astype(o_ref.dtype)
        return carry

    lax.fori_loop(0, nch, pass2, 0, unroll=unroll)


def _gla(q, k, v, r, g, nw, n_ctx):
    bsz, t, _ = q.shape
    nch = t // GLA_CHUNK
    full = lambda width: pl.BlockSpec((1, t, width), lambda b: (b, 0, 0))
    return pl.pallas_call(
        functools.partial(_gla_kernel, n_ctx),
        grid=(bsz,),
        in_specs=[full(GLA_KW), full(GLA_KW), full(GLA_VW), full(GLA_VW), full(2 * GLA_KW),
                  pl.BlockSpec((1, GLA_DV), lambda b: (0, 0))],
        out_specs=full(GLA_VW),
        out_shape=jax.ShapeDtypeStruct((bsz, t, GLA_VW), ACT),
        scratch_shapes=[pltpu.VMEM((t, 2 * GLA_KW), F32),
                        pltpu.VMEM((nch, 2, GLA_DV, GLA_KW), F32),
                        pltpu.VMEM((nch, 8, GLA_KW), F32)],
        compiler_params=_cparams(("arbitrary",)),
    )(q, k, v, r, g, nw)


def _conv_kernel(n_ctx, h_ref, gb_ref, gc_ref, w_ref, o_ref):
    t = h_ref.shape[1]
    a = gc_ref[0].astype(F32) * h_ref[0].astype(F32)
    rows = lax.broadcasted_iota(jnp.int32, (t, 1), 0)
    prev = jnp.where((rows == 0) | (rows == n_ctx), 0.0, pltpu.roll(a, 1, 0))
    nxt = jnp.where((rows == n_ctx - 1) | (rows == t - 1), 0.0, pltpu.roll(a, t - 1, 0))
    w = w_ref[...]
    y = w[0:1, :] * prev + w[1:2, :] * a + w[2:3, :] * nxt
    o_ref[0] = (gb_ref[0].astype(F32) * y).astype(o_ref.dtype)


def _short_conv(conv3, w, n_ctx):
    bsz, t, _ = conv3.shape
    spec = lambda i: pl.BlockSpec((1, t, CONV_W), lambda b: (b, 0, i))
    return pl.pallas_call(
        functools.partial(_conv_kernel, n_ctx),
        grid=(bsz,),
        in_specs=[spec(0), spec(1), spec(2), pl.BlockSpec(w.shape, lambda b: (0, 0))],
        out_specs=pl.BlockSpec((1, t, CONV_W), lambda b: (b, 0, 0)),
        out_shape=jax.ShapeDtypeStruct((bsz, t, CONV_W), ACT),
        compiler_params=_cparams(("arbitrary",)),
    )(conv3, conv3, conv3, w)


def _rope(x, cos, sin_signed):
    lane = lax.broadcasted_iota(jnp.int32, (1, LANES), 1)
    quarter = HEAD_DIM // 4
    swapped = jnp.where((lane % (2 * quarter)) < quarter,
                        pltpu.roll(x, LANES - quarter, 1), pltpu.roll(x, quarter, 1))
    return x * cos + swapped * sin_signed


def _attn_kernel(n_ctx, nb, sink_ref, q_ref, kv_ref, cosq_ref, sinq_ref, cos_all_ref, sin_all_ref, o_ref,
                 kd_scr, vd_scr, s_scr, p_scr):
    step = pl.program_id(1)

    @pl.when(step == 0)
    def _():
        lane = lax.broadcasted_iota(jnp.int32, (1, LANES), 1)
        first_half = lane < HEAD_DIM
        k = _rope(kv_ref[0, :, :LANES].astype(F32), cos_all_ref[...], sin_all_ref[...])
        v = kv_ref[0, :, LANES:].astype(F32)
        for src, dst in ((k, kd_scr), (v, vd_scr)):
            swapped = pltpu.roll(src, HEAD_DIM, 1)
            dst[0] = jnp.where(first_half, src, swapped).astype(ACT)
            dst[1] = jnp.where(first_half, swapped, src).astype(ACT)

    for sub in range(nb):
        _attn_block(n_ctx, step * nb + sub, sub, sink_ref, q_ref, cosq_ref, sinq_ref, o_ref, kd_scr, vd_scr,
                    s_scr.at[sub], p_scr.at[sub])


def _attn_block(n_ctx, j, sub, sink_ref, q_ref, cosq_ref, sinq_ref, o_ref, kd_scr, vd_scr, s_scr, p_scr):
    t = kd_scr.shape[1]
    blk = ATT_BLOCK
    n_loc = 3 * blk
    n_lat = t - n_ctx
    qrows = slice(sub * blk, (sub + 1) * blk)

    lane = lax.broadcasted_iota(jnp.int32, (1, LANES), 1)
    first_half = lane < HEAD_DIM
    cos = cosq_ref[qrows, :]
    sin = sinq_ref[qrows, :]
    qs = []
    for gi in range(ATT_HEADS // 2):
        qg = _rope(q_ref[0, qrows, gi * LANES:(gi + 1) * LANES].astype(F32), cos, sin)
        qs.append(jnp.where(first_half, qg, 0.0))
        qs.append(jnp.where(first_half, 0.0, qg))
    qst = jnp.concatenate(qs, axis=0).astype(ACT)

    jl = j - n_ctx // blk
    start = jnp.clip((jl - 1) * blk, 0, n_lat - n_loc)
    loc = pl.ds(pl.multiple_of(n_ctx + start, blk), n_loc)
    hrows = ATT_GROUP * blk
    for kv in range(ATT_KV_HEADS):
        rs = slice(kv * hrows, (kv + 1) * hrows)
        s_scr[rs, 0:n_ctx] = _dot_nt(qst[rs, :], kd_scr[kv, 0:n_ctx, :])
        s_scr[rs, n_ctx:] = _dot_nt(qst[rs, :], kd_scr[kv, loc, :])
    q_base = jnp.where(jl >= 0, jl * blk, -(t + WINDOW + blk))
    qpos = q_base + lax.broadcasted_iota(jnp.int32, (blk, n_loc), 0)
    kpos = start + lax.broadcasted_iota(jnp.int32, (blk, n_loc), 1)
    bias = jnp.where(jnp.abs(qpos - kpos) <= WINDOW, 0.0, NEG_INF)
    ones = jnp.ones((n_ctx + n_loc, LANES), ACT)

    sink_exp = []
    for i in range(ATT_HEADS):
        rows = slice(i * blk, (i + 1) * blk)
        sink = sink_ref[i] * LOG2E
        sc = s_scr[rows, 0:n_ctx]
        sl = s_scr[rows, n_ctx:] + bias
        m = jnp.maximum(jnp.maximum(jnp.max(sc, axis=-1, keepdims=True),
                                    jnp.max(sl, axis=-1, keepdims=True)), sink)
        p_scr[rows, 0:n_ctx] = jnp.exp2(sc - m).astype(ACT)
        p_scr[rows, n_ctx:] = jnp.exp2(sl - m).astype(ACT)
        sink_exp.append(jnp.exp2(sink - m))
    outs = []
    for kv in range(ATT_KV_HEADS):
        rs = slice(kv * hrows, (kv + 1) * hrows)
        v_aug = jnp.concatenate(
            [jnp.concatenate([vd_scr[kv, 0:n_ctx, :], vd_scr[kv, loc, :]], axis=0), ones], axis=1)
        o_kv = _dot(p_scr[rs, :], v_aug)
        for g in range(ATT_GROUP):
            i = kv * ATT_GROUP + g
            rows = slice(g * blk, (g + 1) * blk)
            outs.append(o_kv[rows, :LANES] / (o_kv[rows, LANES:] + sink_exp[i]))
    for gi in range(ATT_HEADS // 2):
        o_ref[0, qrows, gi * LANES:(gi + 1) * LANES] = jnp.where(
            first_half, outs[2 * gi], outs[2 * gi + 1]).astype(o_ref.dtype)


def _attention(qa, kva, sink, cos, sin_signed, n_ctx, nb):
    bsz, t, _ = qa.shape
    blk = ATT_BLOCK
    rows = nb * blk
    return pl.pallas_call(
        functools.partial(_attn_kernel, n_ctx, nb),
        grid=(bsz, t // rows),
        in_specs=[pl.BlockSpec(memory_space=pltpu.SMEM),
                  pl.BlockSpec((1, rows, ATT_QW), lambda b, j: (b, j, 0)),
                  pl.BlockSpec((1, t, 2 * ATT_KVW), lambda b, j: (b, 0, 0)),
                  pl.BlockSpec((rows, LANES), lambda b, j: (j, 0)),
                  pl.BlockSpec((rows, LANES), lambda b, j: (j, 0)),
                  pl.BlockSpec((t, LANES), lambda b, j: (0, 0)),
                  pl.BlockSpec((t, LANES), lambda b, j: (0, 0))],
        out_specs=pl.BlockSpec((1, rows, ATT_QW), lambda b, j: (b, j, 0)),
        out_shape=jax.ShapeDtypeStruct((bsz, t, ATT_QW), ACT),
        scratch_shapes=[pltpu.VMEM((ATT_KV_HEADS, t, LANES), ACT),
                        pltpu.VMEM((ATT_KV_HEADS, t, LANES), ACT),
                        pltpu.VMEM((nb, ATT_HEADS * blk, n_ctx + 3 * blk), F32),
                        pltpu.VMEM((nb, ATT_HEADS * blk, n_ctx + 3 * blk), ACT)],
        compiler_params=_cparams(("arbitrary", "arbitrary")),
    )(sink, qa, kva, cos * LOG2E, sin_signed * LOG2E, cos, sin_signed)


def _merge_kernel(n_ctx, bm, nsub, ya_ref, yb_ref, yc_ref, ma_ref, mb_ref, mc_ref, x_ref, mod_ref,
                  wa_ref, wb_ref, wc_ref, wo_ref, lnw_ref, lnb_ref, o_ref):
    sub_rows = bm // nsub
    for sub in range(nsub):
        rows = slice(sub * sub_rows, (sub + 1) * sub_rows)
        row0 = pl.program_id(1) * bm + sub * sub_rows
        m = (_sigmoid(ma_ref[0, rows, :].astype(F32)) * _dot(ya_ref[0, rows, :], wa_ref[0])
             + _sigmoid(mb_ref[0, rows, :].astype(F32)) * _dot(yb_ref[0, rows, :], wb_ref[0])
             + _sigmoid(mc_ref[0, rows, :].astype(F32)) * _dot(yc_ref[0, rows, :], wc_ref[0]))
        mix = _dot(m.astype(ACT), wo_ref[0])
        gate = _row_mod(mod_ref[0, 0], 2, row0, sub_rows, n_ctx)
        o_ref[0, rows, :] = _layer_norm(DEEPNORM_ALPHA * x_ref[0, rows, :] + gate * mix,
                                        lnw_ref[...], lnb_ref[...])


def _merge(ya, yb, yc, gates, xs, layer, modb, wa, wb, wc, wo, lnw, lnb, n_ctx, bm, nsub):
    bsz, t, d = xs.shape
    row = lambda width, i=0: pl.BlockSpec((1, bm, width), lambda b, r: (b, r, i))
    const = lambda a: pl.BlockSpec(a.shape, lambda b, r: (0,) * a.ndim, pipeline_mode=pl.Buffered(1))
    return pl.pallas_call(
        functools.partial(_merge_kernel, n_ctx, bm, nsub),
        grid=(bsz, t // bm),
        in_specs=[row(GLA_VW), row(CONV_W), row(ATT_QW), row(d, 0), row(d, 1), row(d, 2), row(d),
                  _mod_spec(modb, layer),
                  _layer_spec(wa, layer), _layer_spec(wb, layer), _layer_spec(wc, layer),
                  _layer_spec(wo, layer), const(lnw), const(lnb)],
        out_specs=row(d),
        out_shape=jax.ShapeDtypeStruct((bsz, t, d), F32),
        compiler_params=_cparams(("arbitrary", "arbitrary")),
    )(ya, yb, yc, gates, gates, gates, xs, modb, wa, wb, wc, wo, lnw, lnb)


def _ffn_kernel(n_ctx, rb, tn, t, row_off, nx, *refs):
    x_refs = refs[:nx]
    (xp_ref, xn_ref, mod_ref, wup_ref, cw_ref, wd_ref, lnw_ref, lnb_ref, o_ref, h_scr, act_scr) = refs[nx:]
    piece = rb // nx
    d_ff = wd_ref.shape[1]
    r = pl.program_id(1)
    row0 = row_off + r * rb
    ext = rb + 2 * HALO
    edge = xp_ref.shape[1]
    mod = mod_ref[0, 0]

    def modulated(xv, first_row):
        n = xv.shape[0]
        return xv * (1.0 + _row_mod(mod, 4, first_row, n, n_ctx)) + _row_mod(mod, 3, first_row, n, n_ctx)

    def same_sequence(rows, neighbour):
        in_lat = (rows >= n_ctx).astype(jnp.int32)
        return (rows >= 0) & (rows < t) & (in_lat == (neighbour >= n_ctx).astype(jnp.int32))

    rows_p = row0 - edge + lax.broadcasted_iota(jnp.int32, (edge, 1), 0)
    rows_n = row0 + rb + lax.broadcasted_iota(jnp.int32, (edge, 1), 0)
    hp = jnp.where(same_sequence(rows_p, row0), modulated(xp_ref[0], row0 - edge), 0.0)
    hn = jnp.where(same_sequence(rows_n, row0 + rb - 1), modulated(xn_ref[0], row0 + rb), 0.0)
    zpad = jnp.zeros((HALO - edge, xp_ref.shape[2]), F32)
    h_scr[0:HALO, :] = jnp.concatenate([zpad, hp], axis=0).astype(ACT)
    for s, x_ref in enumerate(x_refs):
        h_scr[HALO + s * piece:HALO + (s + 1) * piece, :] = modulated(x_ref[0], row0 + s * piece).astype(ACT)
    h_scr[HALO + rb:ext, :] = jnp.concatenate([hn, zpad], axis=0).astype(ACT)

    def conv3(u, w, keep_prev=None, keep_next=None):
        prev = pltpu.roll(u, 1, 0)
        nxt = pltpu.roll(u, u.shape[0] - 1, 0)
        if keep_prev is not None:
            prev = jnp.where(keep_prev, prev, 0.0)
            nxt = jnp.where(keep_next, nxt, 0.0)
        return w[0:1, :] * prev + w[1:2, :] * u + w[2:3, :] * nxt

    lb = (n_ctx - row_off) % rb if n_ctx > row_off else 0
    fix = HALO + 8
    he = h_scr[...]
    nchunks = d_ff // tn

    def up(n):
        return (_dot(he, wup_ref[0, :, n * tn:(n + 1) * tn]),
                _dot(he, wup_ref[0, :, d_ff + n * tn:d_ff + (n + 1) * tn]))

    for n in range(nchunks):
        cg = slice(n * tn, (n + 1) * tn)
        cv = slice(d_ff + n * tn, d_ff + (n + 1) * tn)
        ug, uv = up(n)
        wg = cw_ref[:, cg]
        wv = cw_ref[:, cv]
        gate = conv3(ug, wg)[HALO:HALO + rb, :]
        val = conv3(uv, wv)[HALO:HALO + rb, :]
        act_scr[:, cg] = (_silu(gate) * val).astype(ACT)
        if lb:
            rows_g = row0 + lb - fix + lax.broadcasted_iota(jnp.int32, (2 * fix, 1), 0)
            win = slice(HALO + lb - fix, HALO + lb + fix)
            g_w = conv3(ug[win, :], wg, rows_g != n_ctx, rows_g != n_ctx - 1)
            v_w = conv3(uv[win, :], wv, rows_g != n_ctx, rows_g != n_ctx - 1)
            a_w = _silu(g_w) * v_w
            act_scr[lb - HALO:lb + HALO, cg] = a_w[fix - HALO:fix + HALO, :].astype(ACT)

    y = _dot(act_scr[...], wd_ref[0])
    for s, x_ref in enumerate(x_refs):
        rows = slice(s * piece, (s + 1) * piece)
        gate2 = _row_mod(mod, 5, row0 + s * piece, piece, n_ctx)
        o_ref[0, rows, :] = _layer_norm(DEEPNORM_ALPHA * x_ref[0] + gate2 * y[rows, :], lnw_ref[...], lnb_ref[...])


def _ffn(xs, layer, modb, w_up, w_conv, w_down, lnw, lnb, n_ctx, rb, tn, row_off=0):
    bsz, t, d = xs.shape
    d_ff = w_down.shape[1]
    edge = 8
    rows_out = t - row_off
    assert rows_out % rb == 0 and row_off % edge == 0
    lb = (n_ctx - row_off) % rb if n_ctx > row_off else 0
    assert lb == 0 or (2 * HALO <= lb <= rb - 2 * HALO and lb % HALO == 0)
    piece = int(np.gcd(rb, row_off)) if row_off else rb
    nx = rb // piece
    nb = rb // edge
    first = row_off // edge
    resident = lambda a: pl.BlockSpec(a.shape, lambda b, r: (0,) * a.ndim, pipeline_mode=pl.Buffered(1))
    x_piece = lambda s: pl.BlockSpec((1, piece, d), lambda b, r: (b, r * nx + s + row_off // piece, 0))
    return pl.pallas_call(
        functools.partial(_ffn_kernel, n_ctx, rb, tn, t, row_off, nx),
        grid=(bsz, rows_out // rb),
        in_specs=[x_piece(s) for s in range(nx)] + [
            pl.BlockSpec((1, edge, d), lambda b, r: (b, jnp.maximum(first + r * nb - 1, 0), 0)),
            pl.BlockSpec((1, edge, d), lambda b, r: (b, jnp.minimum(first + (r + 1) * nb, t // edge - 1), 0)),
            _mod_spec(modb, layer),
            _layer_spec(w_up, layer), resident(w_conv), _layer_spec(w_down, layer), resident(lnw), resident(lnb)],
        out_specs=pl.BlockSpec((1, rb, d), lambda b, r: (b, r, 0)),
        out_shape=jax.ShapeDtypeStruct((bsz, rows_out, d), F32),
        scratch_shapes=[pltpu.VMEM((rb + 2 * HALO, d), ACT), pltpu.VMEM((rb, d_ff), ACT)],
        compiler_params=_cparams(("arbitrary", "arbitrary")),
    )(*([xs] * (nx + 2)), modb, w_up, w_conv, w_down, lnw, lnb)


def _rope_tables(n_ctx, seq):
    rows = seq // GRID_W
    row_idx = np.repeat(np.arange(rows), GRID_W).astype(np.float32)
    col_idx = np.tile(np.arange(GRID_W), rows).astype(np.float32)
    half = HEAD_DIM // 2
    inv_freq = (ROPE_THETA ** (-np.arange(0, half, 2, dtype=np.float32) / half)).astype(np.float32)
    ang_r = jnp.asarray(row_idx[:, None] * inv_freq)
    ang_c = jnp.asarray(col_idx[:, None] * inv_freq)
    cos = jnp.concatenate([jnp.cos(ang_r)] * 2 + [jnp.cos(ang_c)] * 2, axis=-1)
    sin = jnp.concatenate([-jnp.sin(ang_r), jnp.sin(ang_r), -jnp.sin(ang_c), jnp.sin(ang_c)], axis=-1)
    cos = jnp.concatenate([jnp.ones((n_ctx, HEAD_DIM), F32), cos], axis=0)
    sin = jnp.concatenate([jnp.zeros((n_ctx, HEAD_DIM), F32), sin], axis=0)
    return jnp.tile(cos, (1, 2)), jnp.tile(sin, (1, 2))


def _largest_divisor(n, cap, multiple):
    best = multiple
    for cand in range(multiple, cap + 1, multiple):
        if n % cand == 0:
            best = cand
    return best


def kernel(x, c, ctx, c_ctx, w_ada, b_ada, w_in, gla_gate_up_f, gla_gate_bias_f, gla_gate_up_b,
           gla_gate_bias_b, gla_norm_w, conv_w, att_sink, w_branch_a, w_branch_b, w_branch_c, w_out,
           ln1_w, ln1_b, ffn_up, ffn_conv, ffn_down, ln2_w, ln2_b):
    bsz, seq, d = x.shape
    n_ctx = ctx.shape[1]
    depth = w_in.shape[0]
    t = n_ctx + seq
    assert n_ctx % GLA_CHUNK == 0 and seq % GLA_CHUNK == 0 and seq >= 3 * ATT_BLOCK
    rb = _largest_divisor(t, 768, 128)
    rb_last = _largest_divisor(seq, 768, 128)
    bm = rb
    nsub = 2 if bm % 256 == 0 else 1
    d_ff = ffn_down.shape[1]
    tn = 256
    nb_att = 2 if (t // ATT_BLOCK) % 2 == 0 else 1

    cvec = jnp.zeros((16, d), F32).at[:bsz].set(c).at[bsz].set(c_ctx)
    mod = _modulation(cvec, w_ada, b_ada)
    mod_lat = mod[:, :bsz].reshape(depth, bsz, 6, d)
    mod_ctx = jnp.broadcast_to(mod[:, bsz].reshape(depth, 1, 6, d), (depth, bsz, 6, d))
    modb = jnp.concatenate([mod_ctx, mod_lat, jnp.zeros((depth, bsz, 4, d), F32)], axis=2)

    assert w_in.shape[2] == _OFF[-1]
    w_in_t = jnp.swapaxes(w_in, 1, 2).astype(ACT)
    zpad = jnp.zeros((depth, GATE_RANK, GLA_KW), F32)
    w_gate = jnp.concatenate(
        [jnp.concatenate([gla_gate_up_f, zpad], axis=2), jnp.concatenate([zpad, gla_gate_up_b], axis=2)],
        axis=1)
    b_gate = jnp.concatenate([gla_gate_bias_f, gla_gate_bias_b], axis=1).reshape(depth, 1, 2 * GLA_KW)

    cos, sin_signed = _rope_tables(n_ctx, seq)
    nw = gla_norm_w.reshape(depth, 1, GLA_DV)
    wa = w_branch_a.astype(ACT)
    wb = w_branch_b.astype(ACT)
    wc = w_branch_c.astype(ACT)
    wo = w_out.astype(ACT)
    w_up = ffn_up.astype(ACT)
    w_dn = ffn_down.astype(ACT)

    xs = None
    for l in range(depth):
        last = l == depth - 1
        if l == 0 and bm % n_ctx == 0:
            *proj, xs = _inproj(None, l, modb, w_in_t, w_gate, b_gate, n_ctx, bm, nsub, parts=(ctx, x))
        else:
            if xs is None:
                xs = jnp.concatenate([ctx, x], axis=1)
            proj = _inproj(xs, l, modb, w_in_t, w_gate, b_gate, n_ctx, bm, nsub)
        k, v, kva, q, r, conv3, qa, gates, g = proj
        ya = _gla(q, k, v, r, g, nw[l], n_ctx)
        yb = _short_conv(conv3, conv_w[l], n_ctx)
        yc = _attention(qa, kva, att_sink[l], cos, sin_signed, n_ctx, nb_att)
        xs = _merge(ya, yb, yc, gates, xs, l, modb, wa, wb, wc, wo,
                    ln1_w[l].reshape(1, d), ln1_b[l].reshape(1, d), n_ctx, bm, nsub)
        xs = _ffn(xs, l, modb, w_up, ffn_conv[l], w_dn, ln2_w[l].reshape(1, d), ln2_b[l].reshape(1, d),
                  n_ctx, rb_last if last else rb, tn, row_off=n_ctx if last else 0)
    return xs
```

```python
import functools

import numpy as np
import jax
import jax.numpy as jnp
from jax import lax
from jax.experimental import pallas as pl
from jax.experimental.pallas import tpu as pltpu

DEPTH_FOR_DEEPNORM = 4
DEEPNORM_ALPHA = (2 * DEPTH_FOR_DEEPNORM) ** 0.25
LN_EPS = 1e-5
RMS_EPS = 1e-6
GATE_TAU = 16.0
NEG_INF = -1e30
ROPE_THETA = 10000.0
LOG2E = 1.4426950408889634

GRID_W = 64
GLA_HEADS = 4
GLA_DK = 64
GLA_DV = 128
GLA_KW = GLA_HEADS * GLA_DK
GLA_VW = GLA_HEADS * GLA_DV
GATE_RANK = 16
CONV_W = 512
ATT_HEADS = 8
ATT_KV_HEADS = 2
ATT_GROUP = ATT_HEADS // ATT_KV_HEADS
HEAD_DIM = 64
ATT_QW = ATT_HEADS * HEAD_DIM
ATT_KVW = ATT_KV_HEADS * HEAD_DIM
WINDOW = 128
ATT_BLOCK = 128

LANES = 128
GLA_CHUNK = 128
HALO = 16
VMEM_LIMIT = 56 * 1024 * 1024

ACT = jnp.bfloat16
F32 = jnp.float32


def _cparams(sem):
    return pltpu.CompilerParams(dimension_semantics=sem, vmem_limit_bytes=VMEM_LIMIT)


def _split_hi_lo(a):
    hi = a.astype(ACT)
    lo = (a - hi.astype(F32)).astype(ACT)
    return hi, lo


def _dot(a, b):
    return jnp.dot(a, b, preferred_element_type=F32)


def _dot_nt(a, b):
    return lax.dot_general(a, b, (((1,), (1,)), ((), ())), preferred_element_type=F32)


def _dot3(a, b):
    ah, al = _split_hi_lo(a)
    bh, bl = _split_hi_lo(b)
    return _dot(ah, bh) + _dot(ah, bl) + _dot(al, bh)


def _sigmoid(x):
    return 1.0 / (1.0 + jnp.exp(-x))


def _silu(x):
    return x * _sigmoid(x)


def _log_sigmoid(x):
    return jnp.minimum(x, 0.0) - jnp.log(1.0 + jnp.exp(-jnp.abs(x)))


def _layer_norm(y, w, b):
    mu = jnp.mean(y, axis=-1, keepdims=True)
    yc = y - mu
    var = jnp.mean(yc * yc, axis=-1, keepdims=True)
    return yc * lax.rsqrt(var + LN_EPS) * w + b


def _row_mod(mod, idx, row0, nrows, n_ctx):
    rows = row0 + lax.broadcasted_iota(jnp.int32, (nrows, 1), 0)
    return jnp.where(rows < n_ctx, mod[idx:idx + 1, :], mod[6 + idx:7 + idx, :])


def _mod_kernel(c_ref, w_ref, b_ref, o_ref):
    s = _silu(c_ref[...])
    o_ref[0] = _dot3(s, w_ref[0]) + b_ref[0]


def _modulation(cvec, w_ada, b_ada):
    depth, d, n = w_ada.shape
    tn = n // 4
    rows = cvec.shape[0]
    return pl.pallas_call(
        _mod_kernel,
        grid=(depth, n // tn),
        in_specs=[pl.BlockSpec((rows, d), lambda l, j: (0, 0)),
                  pl.BlockSpec((1, d, tn), lambda l, j: (l, 0, j)),
                  pl.BlockSpec((1, 1, tn), lambda l, j: (l, 0, j))],
        out_specs=pl.BlockSpec((1, rows, tn), lambda l, j: (l, 0, j)),
        out_shape=jax.ShapeDtypeStruct((depth, rows, n), F32),
        compiler_params=_cparams(("arbitrary", "arbitrary")),
    )(cvec, w_ada, b_ada.reshape(depth, 1, n))


_W_K, _W_V, _W_GLR, _W_KVA, _W_Q, _W_R, _W_CONV, _W_QA, _W_GATES = (
    GLA_KW, GLA_VW, 2 * GATE_RANK, 2 * ATT_KVW, GLA_KW, GLA_VW, 3 * CONV_W, ATT_QW, 3 * 1024)
_OFF = np.cumsum([0, _W_K, _W_V, _W_GLR, _W_KVA, _W_Q, _W_R, _W_CONV, _W_QA, _W_GATES]).tolist()


def _layer_spec(a, layer):
    tail = (0,) * (a.ndim - 1)
    return pl.BlockSpec((1,) + a.shape[1:], lambda *_: (layer,) + tail, pipeline_mode=pl.Buffered(1))


def _mod_spec(modb, layer):
    return pl.BlockSpec((1, 1) + modb.shape[2:], lambda b, *_: (layer, b, 0, 0))


def _inproj_kernel(n_ctx, bm, nsub, from_parts, *refs):
    if from_parts:
        npieces = bm // n_ctx
        ctx_ref, piece_refs, refs = refs[0], refs[1:1 + npieces], refs[1 + npieces:]
        x_ref = refs[-1]
        refs = refs[:-1]
        for s, p_ref in enumerate(piece_refs):
            rows_g = pl.program_id(1) * bm + s * n_ctx + lax.broadcasted_iota(jnp.int32, (n_ctx, 1), 0)
            x_ref[0, s * n_ctx:(s + 1) * n_ctx, :] = jnp.where(rows_g < n_ctx, ctx_ref[0], p_ref[0])
    else:
        x_ref, refs = refs[0], refs[1:]
    (mod_ref, wt_ref, wg_ref, bg_ref,
     k_ref, v_ref, kva_ref, q_ref, r_ref, conv_ref, qa_ref, gates_ref, g_ref) = refs
    mod = mod_ref[0, 0]
    sub_rows = bm // nsub
    for sub in range(nsub):
        rows = slice(sub * sub_rows, (sub + 1) * sub_rows)
        row0 = pl.program_id(1) * bm + sub * sub_rows
        sh = _row_mod(mod, 0, row0, sub_rows, n_ctx)
        sc = _row_mod(mod, 1, row0, sub_rows, n_ctx)
        h = (x_ref[0, rows, :] * (1.0 + sc) + sh).astype(ACT)

        def proj(c0, width):
            return _dot_nt(h, wt_ref[0, c0:c0 + width, :])

        z = _dot3(proj(_OFF[2], _W_GLR), wg_ref[0]) + bg_ref[0]
        g_ref[0, rows, :] = _log_sigmoid(z) * (1.0 / GATE_TAU)
        k_ref[0, rows, :] = proj(_OFF[0], _W_K).astype(ACT)
        v_ref[0, rows, :] = proj(_OFF[1], _W_V).astype(ACT)
        kva_ref[0, rows, :] = proj(_OFF[3], _W_KVA).astype(ACT)
        q_ref[0, rows, :] = (proj(_OFF[4], _W_Q) * (GLA_DK ** -0.5)).astype(ACT)
        r_ref[0, rows, :] = proj(_OFF[5], _W_R).astype(ACT)
        for i in range(_W_CONV // 512):
            conv_ref[0, rows, i * 512:(i + 1) * 512] = proj(_OFF[6] + i * 512, 512).astype(ACT)
        qa_ref[0, rows, :] = (proj(_OFF[7], _W_QA) * (HEAD_DIM ** -0.5)).astype(ACT)
        for i in range(_W_GATES // 1024):
            gates_ref[0, rows, i * 1024:(i + 1) * 1024] = proj(_OFF[8] + i * 1024, 1024).astype(ACT)


def _inproj(xs, layer, modb, wt, wg, bg, n_ctx, bm, nsub, parts=None):
    if parts is None:
        bsz, t, d = xs.shape
    else:
        ctx, x = parts
        bsz, seq, d = x.shape
        t = n_ctx + seq
        assert bm % n_ctx == 0
    row = lambda width: pl.BlockSpec((1, bm, width), lambda b, r: (b, r, 0))
    out = lambda width, dt: jax.ShapeDtypeStruct((bsz, t, width), dt)
    if parts is None:
        x_specs, x_args = [row(d)], [xs]
    else:
        npieces = bm // n_ctx
        piece = lambda s: pl.BlockSpec((1, n_ctx, d), lambda b, r: (b, jnp.maximum(r * npieces + s - 1, 0), 0))
        x_specs = [pl.BlockSpec((1, n_ctx, d), lambda b, r: (b, 0, 0))] + [piece(s) for s in range(npieces)]
        x_args = [ctx] + [x] * npieces
    out_specs = [row(_W_K), row(_W_V), row(_W_KVA), row(_W_Q), row(_W_R), row(_W_CONV), row(_W_QA),
                 row(_W_GATES), row(2 * GLA_KW)]
    out_shape = [out(_W_K, ACT), out(_W_V, ACT), out(_W_KVA, ACT), out(_W_Q, ACT), out(_W_R, ACT),
                 out(_W_CONV, ACT), out(_W_QA, ACT), out(_W_GATES, ACT), out(2 * GLA_KW, F32)]
    if parts is not None:
        out_specs.append(row(d))
        out_shape.append(out(d, F32))
    return pl.pallas_call(
        functools.partial(_inproj_kernel, n_ctx, bm, nsub, parts is not None),
        grid=(bsz, t // bm),
        in_specs=x_specs + [_mod_spec(modb, layer),
                            _layer_spec(wt, layer), _layer_spec(wg, layer), _layer_spec(bg, layer)],
        out_specs=out_specs,
        out_shape=out_shape,
        compiler_params=_cparams(("arbitrary", "arbitrary")),
    )(*x_args, modb, wt, wg, bg)


def _gla_kernel(n_ctx, q_ref, k_ref, v_ref, r_ref, g_ref, nw_ref, o_ref, b_scr, u_scr, dec_scr):
    t = q_ref.shape[1]
    L = GLA_CHUNK
    nch = t // L
    nch_ctx = n_ctx // L
    kw = GLA_KW

    ri = lax.broadcasted_iota(jnp.int32, (L, L), 0)
    ci = lax.broadcasted_iota(jnp.int32, (L, L), 1)
    tri_f = (ci <= ri).astype(ACT)
    tri_b = (ci >= ri).astype(ACT)
    lane_k = lax.broadcasted_iota(jnp.int32, (1, kw), 1)
    head_masks = [(lane_k >= h * GLA_DK) & (lane_k < (h + 1) * GLA_DK) for h in range(GLA_HEADS)]

    def rows(c):
        return pl.ds(pl.multiple_of(c * L, L), L)

    def pass1(c, carry):
        rs = rows(c)
        g = g_ref[0, rs, :]
        g_hi, g_lo = _split_hi_lo(g)
        b_f = _dot(tri_f, g_hi[:, :kw]) + _dot(tri_f, g_lo[:, :kw])
        b_b = _dot(tri_b, g_hi[:, kw:]) + _dot(tri_b, g_lo[:, kw:])
        b_scr[rs, :kw] = b_f
        b_scr[rs, kw:] = b_b
        tot_f = b_f[L - 1:L, :]
        tot_b = b_b[0:1, :]
        k = k_ref[0, rs, :].astype(F32)
        ku_f = (k * jnp.exp(tot_f - b_f)).astype(ACT)
        ku_b = (k * jnp.exp(tot_b - b_b)).astype(ACT)
        ku = jnp.concatenate([ku_f, ku_b], axis=1)
        vt = v_ref[0, rs, :].astype(F32).T.astype(ACT)
        u_full = _dot(vt, ku)
        u_f = jnp.zeros((GLA_DV, kw), F32)
        u_b = jnp.zeros((GLA_DV, kw), F32)
        for h in range(GLA_HEADS):
            blk = u_full[h * GLA_DV:(h + 1) * GLA_DV, :]
            u_f = u_f + jnp.where(head_masks[h], blk[:, :kw], 0.0)
            u_b = u_b + jnp.where(head_masks[h], blk[:, kw:], 0.0)
        u_scr[c, 0] = u_f
        u_scr[c, 1] = u_b
        dec_scr[c, 0:1, :] = jnp.exp(tot_f)
        dec_scr[c, 1:2, :] = jnp.exp(tot_b)
        return carry

    unroll = 6 if nch % 6 == 0 else 1
    lax.fori_loop(0, nch, pass1, 0, unroll=unroll)

    def scan_f(c, s):
        u = u_scr[c, 0]
        u_scr[c, 0] = s
        return s * dec_scr[c, 0:1, :] + u

    lax.fori_loop(0, nch, scan_f, jnp.zeros((GLA_DV, kw), F32))

    def scan_b(i, s):
        c = jnp.where(i < nch_ctx, nch_ctx - 1 - i, nch - 1 - (i - nch_ctx))
        u = u_scr[c, 1]
        u_scr[c, 1] = s
        return s * dec_scr[c, 1:2, :] + u

    lax.fori_loop(0, nch, scan_b, jnp.zeros((GLA_DV, kw), F32))

    ri_st = lax.broadcasted_iota(jnp.int32, (GLA_HEADS * L, L), 0) & (L - 1)
    ci_st = lax.broadcasted_iota(jnp.int32, (GLA_HEADS * L, L), 1)
    tril = ci_st <= ri_st
    triu = ci_st >= ri_st
    nw = nw_ref[...]

    def pass2(c, carry):
        rs = rows(c)
        q = q_ref[0, rs, :].astype(F32)
        k = k_ref[0, rs, :].astype(F32)
        v = v_ref[0, rs, :]
        b_f = b_scr[rs, :kw]
        b_b = b_scr[rs, kw:]
        e_f = jnp.exp(b_f)
        e_b = jnp.exp(b_b)
        qd_f = q * e_f
        qd_b = q * e_b
        ki_f = (k / e_f).astype(ACT)
        ki_b = (k / e_b).astype(ACT)
        qs_f = jnp.concatenate([jnp.where(m, qd_f, 0.0) for m in head_masks], axis=0).astype(ACT)
        qs_b = jnp.concatenate([jnp.where(m, qd_b, 0.0) for m in head_masks], axis=0).astype(ACT)
        p = (jnp.where(tril, _dot_nt(qs_f, ki_f), 0.0)
             + jnp.where(triu, _dot_nt(qs_b, ki_b), 0.0)).astype(ACT)
        s_f = u_scr[c, 0].astype(ACT)
        s_b = u_scr[c, 1].astype(ACT)
        inter = _dot_nt(qs_f, s_f) + _dot_nt(qs_b, s_b)
        outs = []
        for h in range(GLA_HEADS):
            o_h = inter[h * L:(h + 1) * L, :] + _dot(p[h * L:(h + 1) * L, :], v[:, h * GLA_DV:(h + 1) * GLA_DV])
            ms = jnp.mean(o_h * o_h, axis=-1, keepdims=True)
            outs.append(o_h * lax.rsqrt(ms + RMS_EPS) * nw)
        o = jnp.concatenate(outs, axis=1)
        o_ref[0, rs, :] = (o * _silu(r_ref[0, rs, :].astype(F32))).astype(o_ref.dtype)
        return carry

    lax.fori_loop(0, nch, pass2, 0, unroll=unroll)


def _gla(q, k, v, r, g, nw, n_ctx):
    bsz, t, _ = q.shape
    nch = t // GLA_CHUNK
    full = lambda width: pl.BlockSpec((1, t, width), lambda b: (b, 0, 0))
    return pl.pallas_call(
        functools.partial(_gla_kernel, n_ctx),
        grid=(bsz,),
        in_specs=[full(GLA_KW), full(GLA_KW), full(GLA_VW), full(GLA_VW), full(2 * GLA_KW),
                  pl.BlockSpec((1, GLA_DV), lambda b: (0, 0))],
        out_specs=full(GLA_VW),
        out_shape=jax.ShapeDtypeStruct((bsz, t, GLA_VW), ACT),
        scratch_shapes=[pltpu.VMEM((t, 2 * GLA_KW), F32),
                        pltpu.VMEM((nch, 2, GLA_DV, GLA_KW), F32),
                        pltpu.VMEM((nch, 8, GLA_KW), F32)],
        compiler_params=_cparams(("arbitrary",)),
    )(q, k, v, r, g, nw)


def _conv_kernel(n_ctx, h_ref, gb_ref, gc_ref, w_ref, o_ref):
    t = h_ref.shape[1]
    a = gc_ref[0].astype(F32) * h_ref[0].astype(F32)
    rows = lax.broadcasted_iota(jnp.int32, (t, 1), 0)
    prev = jnp.where((rows == 0) | (rows == n_ctx), 0.0, pltpu.roll(a, 1, 0))
    nxt = jnp.where((rows == n_ctx - 1) | (rows == t - 1), 0.0, pltpu.roll(a, t - 1, 0))
    w = w_ref[...]
    y = w[0:1, :] * prev + w[1:2, :] * a + w[2:3, :] * nxt
    o_ref[0] = (gb_ref[0].astype(F32) * y).astype(o_ref.dtype)


def _short_conv(conv3, w, n_ctx):
    bsz, t, _ = conv3.shape
    spec = lambda i: pl.BlockSpec((1, t, CONV_W), lambda b: (b, 0, i))
    return pl.pallas_call(
        functools.partial(_conv_kernel, n_ctx),
        grid=(bsz,),
        in_specs=[spec(0), spec(1), spec(2), pl.BlockSpec(w.shape, lambda b: (0, 0))],
        out_specs=pl.BlockSpec((1, t, CONV_W), lambda b: (b, 0, 0)),
        out_shape=jax.ShapeDtypeStruct((bsz, t, CONV_W), ACT),
        compiler_params=_cparams(("arbitrary",)),
    )(conv3, conv3, conv3, w)


def _rope(x, cos, sin_signed):
    lane = lax.broadcasted_iota(jnp.int32, (1, LANES), 1)
    quarter = HEAD_DIM // 4
    swapped = jnp.where((lane % (2 * quarter)) < quarter,
                        pltpu.roll(x, LANES - quarter, 1), pltpu.roll(x, quarter, 1))
    return x * cos + swapped * sin_signed


def _attn_kernel(n_ctx, nb, sink_ref, q_ref, kv_ref, cosq_ref, sinq_ref, cos_all_ref, sin_all_ref, o_ref,
                 kd_scr, vd_scr, s_scr, p_scr):
    step = pl.program_id(1)

    @pl.when(step == 0)
    def _():
        lane = lax.broadcasted_iota(jnp.int32, (1, LANES), 1)
        first_half = lane < HEAD_DIM
        k = _rope(kv_ref[0, :, :LANES].astype(F32), cos_all_ref[...], sin_all_ref[...])
        v = kv_ref[0, :, LANES:].astype(F32)
        for src, dst in ((k, kd_scr), (v, vd_scr)):
            swapped = pltpu.roll(src, HEAD_DIM, 1)
            dst[0] = jnp.where(first_half, src, swapped).astype(ACT)
            dst[1] = jnp.where(first_half, swapped, src).astype(ACT)

    blocks = [_attn_block(n_ctx, step * nb + sub, sub, sink_ref, q_ref, cosq_ref, sinq_ref, o_ref, kd_scr, vd_scr,
                          s_scr.at[sub], p_scr.at[sub]) for sub in range(nb)]
    for _ in range(3):
        for blk_stages in blocks:
            next(blk_stages, None)


def _attn_block(n_ctx, j, sub, sink_ref, q_ref, cosq_ref, sinq_ref, o_ref, kd_scr, vd_scr, s_scr, p_scr):
    t = kd_scr.shape[1]
    blk = ATT_BLOCK
    n_loc = 3 * blk
    n_lat = t - n_ctx
    qrows = slice(sub * blk, (sub + 1) * blk)

    lane = lax.broadcasted_iota(jnp.int32, (1, LANES), 1)
    first_half = lane < HEAD_DIM
    cos = cosq_ref[qrows, :]
    sin = sinq_ref[qrows, :]
    qs = []
    for gi in range(ATT_HEADS // 2):
        qg = _rope(q_ref[0, qrows, gi * LANES:(gi + 1) * LANES].astype(F32), cos, sin)
        qs.append(jnp.where(first_half, qg, 0.0))
        qs.append(jnp.where(first_half, 0.0, qg))
    qst = jnp.concatenate(qs, axis=0).astype(ACT)

    jl = j - n_ctx // blk
    start = jnp.clip((jl - 1) * blk, 0, n_lat - n_loc)
    loc = pl.ds(pl.multiple_of(n_ctx + start, blk), n_loc)
    hrows = ATT_GROUP * blk
    for kv in range(ATT_KV_HEADS):
        rs = slice(kv * hrows, (kv + 1) * hrows)
        s_scr[rs, 0:n_ctx] = _dot_nt(qst[rs, :], kd_scr[kv, 0:n_ctx, :])
        s_scr[rs, n_ctx:] = _dot_nt(qst[rs, :], kd_scr[kv, loc, :])
    q_base = jnp.where(jl >= 0, jl * blk, -(t + WINDOW + blk))
    qpos = q_base + lax.broadcasted_iota(jnp.int32, (blk, n_loc), 0)
    kpos = start + lax.broadcasted_iota(jnp.int32, (blk, n_loc), 1)
    bias = jnp.where(jnp.abs(qpos - kpos) <= WINDOW, 0.0, NEG_INF)
    ones = jnp.ones((n_ctx + n_loc, LANES), ACT)
    yield

    sink_exp = []
    for i in range(ATT_HEADS):
        rows = slice(i * blk, (i + 1) * blk)
        sink = sink_ref[i] * LOG2E
        sc = s_scr[rows, 0:n_ctx]
        sl = s_scr[rows, n_ctx:] + bias
        m = jnp.maximum(jnp.maximum(jnp.max(sc, axis=-1, keepdims=True),
                                    jnp.max(sl, axis=-1, keepdims=True)), sink)
        p_scr[rows, 0:n_ctx] = jnp.exp2(sc - m).astype(ACT)
        p_scr[rows, n_ctx:] = jnp.exp2(sl - m).astype(ACT)
        sink_exp.append(jnp.exp2(sink - m))
    yield
    outs = []
    for kv in range(ATT_KV_HEADS):
        rs = slice(kv * hrows, (kv + 1) * hrows)
        v_aug = jnp.concatenate(
            [jnp.concatenate([vd_scr[kv, 0:n_ctx, :], vd_scr[kv, loc, :]], axis=0), ones], axis=1)
        o_kv = _dot(p_scr[rs, :], v_aug)
        for g in range(ATT_GROUP):
            i = kv * ATT_GROUP + g
            rows = slice(g * blk, (g + 1) * blk)
            outs.append(o_kv[rows, :LANES] / (o_kv[rows, LANES:] + sink_exp[i]))
    for gi in range(ATT_HEADS // 2):
        o_ref[0, qrows, gi * LANES:(gi + 1) * LANES] = jnp.where(
            first_half, outs[2 * gi], outs[2 * gi + 1]).astype(o_ref.dtype)


def _attention(qa, kva, sink, cos, sin_signed, n_ctx, nb):
    bsz, t, _ = qa.shape
    blk = ATT_BLOCK
    rows = nb * blk
    return pl.pallas_call(
        functools.partial(_attn_kernel, n_ctx, nb),
        grid=(bsz, t // rows),
        in_specs=[pl.BlockSpec(memory_space=pltpu.SMEM),
                  pl.BlockSpec((1, rows, ATT_QW), lambda b, j: (b, j, 0)),
                  pl.BlockSpec((1, t, 2 * ATT_KVW), lambda b, j: (b, 0, 0)),
                  pl.BlockSpec((rows, LANES), lambda b, j: (j, 0)),
                  pl.BlockSpec((rows, LANES), lambda b, j: (j, 0)),
                  pl.BlockSpec((t, LANES), lambda b, j: (0, 0)),
                  pl.BlockSpec((t, LANES), lambda b, j: (0, 0))],
        out_specs=pl.BlockSpec((1, rows, ATT_QW), lambda b, j: (b, j, 0)),
        out_shape=jax.ShapeDtypeStruct((bsz, t, ATT_QW), ACT),
        scratch_shapes=[pltpu.VMEM((ATT_KV_HEADS, t, LANES), ACT),
                        pltpu.VMEM((ATT_KV_HEADS, t, LANES), ACT),
                        pltpu.VMEM((nb, ATT_HEADS * blk, n_ctx + 3 * blk), F32),
                        pltpu.VMEM((nb, ATT_HEADS * blk, n_ctx + 3 * blk), ACT)],
        compiler_params=_cparams(("arbitrary", "arbitrary")),
    )(sink, qa, kva, cos * LOG2E, sin_signed * LOG2E, cos, sin_signed)


def _merge_kernel(n_ctx, bm, nsub, ya_ref, yb_ref, yc_ref, ma_ref, mb_ref, mc_ref, x_ref, mod_ref,
                  wa_ref, wb_ref, wc_ref, wo_ref, lnw_ref, lnb_ref, o_ref):
    sub_rows = bm // nsub
    parts = [slice(sub * sub_rows, (sub + 1) * sub_rows) for sub in range(nsub)]
    merged = []
    for rows in parts:
        m = (_sigmoid(ma_ref[0, rows, :].astype(F32)) * _dot(ya_ref[0, rows, :], wa_ref[0])
             + _sigmoid(mb_ref[0, rows, :].astype(F32)) * _dot(yb_ref[0, rows, :], wb_ref[0])
             + _sigmoid(mc_ref[0, rows, :].astype(F32)) * _dot(yc_ref[0, rows, :], wc_ref[0]))
        merged.append(m.astype(ACT))
    mixes = [_dot(m, wo_ref[0]) for m in merged]
    for sub, (rows, mix) in enumerate(zip(parts, mixes)):
        row0 = pl.program_id(1) * bm + sub * sub_rows
        gate = _row_mod(mod_ref[0, 0], 2, row0, sub_rows, n_ctx)
        o_ref[0, rows, :] = _layer_norm(DEEPNORM_ALPHA * x_ref[0, rows, :] + gate * mix,
                                        lnw_ref[...], lnb_ref[...])


def _merge(ya, yb, yc, gates, xs, layer, modb, wa, wb, wc, wo, lnw, lnb, n_ctx, bm, nsub):
    bsz, t, d = xs.shape
    row = lambda width, i=0: pl.BlockSpec((1, bm, width), lambda b, r: (b, r, i))
    const = lambda a: pl.BlockSpec(a.shape, lambda b, r: (0,) * a.ndim, pipeline_mode=pl.Buffered(1))
    return pl.pallas_call(
        functools.partial(_merge_kernel, n_ctx, bm, nsub),
        grid=(bsz, t // bm),
        in_specs=[row(GLA_VW), row(CONV_W), row(ATT_QW), row(d, 0), row(d, 1), row(d, 2), row(d),
                  _mod_spec(modb, layer),
                  _layer_spec(wa, layer), _layer_spec(wb, layer), _layer_spec(wc, layer),
                  _layer_spec(wo, layer), const(lnw), const(lnb)],
        out_specs=row(d),
        out_shape=jax.ShapeDtypeStruct((bsz, t, d), F32),
        compiler_params=_cparams(("arbitrary", "arbitrary")),
    )(ya, yb, yc, gates, gates, gates, xs, modb, wa, wb, wc, wo, lnw, lnb)


def _ffn_kernel(n_ctx, rb, tn, t, row_off, nx, *refs):
    x_refs = refs[:nx]
    (xp_ref, xn_ref, mod_ref, wup_ref, cw_ref, wd_ref, lnw_ref, lnb_ref, o_ref, h_scr, act_scr) = refs[nx:]
    piece = rb // nx
    d_ff = wd_ref.shape[1]
    r = pl.program_id(1)
    row0 = row_off + r * rb
    ext = rb + 2 * HALO
    edge = xp_ref.shape[1]
    mod = mod_ref[0, 0]

    def modulated(xv, first_row):
        n = xv.shape[0]
        return xv * (1.0 + _row_mod(mod, 4, first_row, n, n_ctx)) + _row_mod(mod, 3, first_row, n, n_ctx)

    def same_sequence(rows, neighbour):
        in_lat = (rows >= n_ctx).astype(jnp.int32)
        return (rows >= 0) & (rows < t) & (in_lat == (neighbour >= n_ctx).astype(jnp.int32))

    rows_p = row0 - edge + lax.broadcasted_iota(jnp.int32, (edge, 1), 0)
    rows_n = row0 + rb + lax.broadcasted_iota(jnp.int32, (edge, 1), 0)
    hp = jnp.where(same_sequence(rows_p, row0), modulated(xp_ref[0], row0 - edge), 0.0)
    hn = jnp.where(same_sequence(rows_n, row0 + rb - 1), modulated(xn_ref[0], row0 + rb), 0.0)
    zpad = jnp.zeros((HALO - edge, xp_ref.shape[2]), F32)
    h_scr[0:HALO, :] = jnp.concatenate([zpad, hp], axis=0).astype(ACT)
    for s, x_ref in enumerate(x_refs):
        h_scr[HALO + s * piece:HALO + (s + 1) * piece, :] = modulated(x_ref[0], row0 + s * piece).astype(ACT)
    h_scr[HALO + rb:ext, :] = jnp.concatenate([hn, zpad], axis=0).astype(ACT)

    def conv3(u, w, keep_prev=None, keep_next=None):
        prev = pltpu.roll(u, 1, 0)
        nxt = pltpu.roll(u, u.shape[0] - 1, 0)
        if keep_prev is not None:
            prev = jnp.where(keep_prev, prev, 0.0)
            nxt = jnp.where(keep_next, nxt, 0.0)
        return w[0:1, :] * prev + w[1:2, :] * u + w[2:3, :] * nxt

    lb = (n_ctx - row_off) % rb if n_ctx > row_off else 0
    fix = HALO + 8
    he = h_scr[...]
    nchunks = d_ff // tn

    def up(n):
        return (_dot(he, wup_ref[0, :, n * tn:(n + 1) * tn]),
                _dot(he, wup_ref[0, :, d_ff + n * tn:d_ff + (n + 1) * tn]))

    for n in range(nchunks):
        cg = slice(n * tn, (n + 1) * tn)
        cv = slice(d_ff + n * tn, d_ff + (n + 1) * tn)
        ug, uv = up(n)
        wg = cw_ref[:, cg]
        wv = cw_ref[:, cv]
        gate = conv3(ug, wg)[HALO:HALO + rb, :]
        val = conv3(uv, wv)[HALO:HALO + rb, :]
        act_scr[:, cg] = (_silu(gate) * val).astype(ACT)
        if lb:
            rows_g = row0 + lb - fix + lax.broadcasted_iota(jnp.int32, (2 * fix, 1), 0)
            win = slice(HALO + lb - fix, HALO + lb + fix)
            g_w = conv3(ug[win, :], wg, rows_g != n_ctx, rows_g != n_ctx - 1)
            v_w = conv3(uv[win, :], wv, rows_g != n_ctx, rows_g != n_ctx - 1)
            a_w = _silu(g_w) * v_w
            act_scr[lb - HALO:lb + HALO, cg] = a_w[fix - HALO:fix + HALO, :].astype(ACT)

    y = _dot(act_scr[...], wd_ref[0])
    for s, x_ref in enumerate(x_refs):
        rows = slice(s * piece, (s + 1) * piece)
        gate2 = _row_mod(mod, 5, row0 + s * piece, piece, n_ctx)
        o_ref[0, rows, :] = _layer_norm(DEEPNORM_ALPHA * x_ref[0] + gate2 * y[rows, :], lnw_ref[...], lnb_ref[...])


def _ffn(xs, layer, modb, w_up, w_conv, w_down, lnw, lnb, n_ctx, rb, tn, row_off=0):
    bsz, t, d = xs.shape
    d_ff = w_down.shape[1]
    edge = 8
    rows_out = t - row_off
    assert rows_out % rb == 0 and row_off % edge == 0
    lb = (n_ctx - row_off) % rb if n_ctx > row_off else 0
    assert lb == 0 or (2 * HALO <= lb <= rb - 2 * HALO and lb % HALO == 0)
    piece = int(np.gcd(rb, row_off)) if row_off else rb
    nx = rb // piece
    nb = rb // edge
    first = row_off // edge
    resident = lambda a: pl.BlockSpec(a.shape, lambda b, r: (0,) * a.ndim, pipeline_mode=pl.Buffered(1))
    x_piece = lambda s: pl.BlockSpec((1, piece, d), lambda b, r: (b, r * nx + s + row_off // piece, 0))
    return pl.pallas_call(
        functools.partial(_ffn_kernel, n_ctx, rb, tn, t, row_off, nx),
        grid=(bsz, rows_out // rb),
        in_specs=[x_piece(s) for s in range(nx)] + [
            pl.BlockSpec((1, edge, d), lambda b, r: (b, jnp.maximum(first + r * nb - 1, 0), 0)),
            pl.BlockSpec((1, edge, d), lambda b, r: (b, jnp.minimum(first + (r + 1) * nb, t // edge - 1), 0)),
            _mod_spec(modb, layer),
            _layer_spec(w_up, layer), resident(w_conv), _layer_spec(w_down, layer), resident(lnw), resident(lnb)],
        out_specs=pl.BlockSpec((1, rb, d), lambda b, r: (b, r, 0)),
        out_shape=jax.ShapeDtypeStruct((bsz, rows_out, d), F32),
        scratch_shapes=[pltpu.VMEM((rb + 2 * HALO, d), ACT), pltpu.VMEM((rb, d_ff), ACT)],
        compiler_params=_cparams(("arbitrary", "arbitrary")),
    )(*([xs] * (nx + 2)), modb, w_up, w_conv, w_down, lnw, lnb)


def _rope_tables(n_ctx, seq):
    rows = seq // GRID_W
    row_idx = np.repeat(np.arange(rows), GRID_W).astype(np.float32)
    col_idx = np.tile(np.arange(GRID_W), rows).astype(np.float32)
    half = HEAD_DIM // 2
    inv_freq = (ROPE_THETA ** (-np.arange(0, half, 2, dtype=np.float32) / half)).astype(np.float32)
    ang_r = jnp.asarray(row_idx[:, None] * inv_freq)
    ang_c = jnp.asarray(col_idx[:, None] * inv_freq)
    cos = jnp.concatenate([jnp.cos(ang_r)] * 2 + [jnp.cos(ang_c)] * 2, axis=-1)
    sin = jnp.concatenate([-jnp.sin(ang_r), jnp.sin(ang_r), -jnp.sin(ang_c), jnp.sin(ang_c)], axis=-1)
    cos = jnp.concatenate([jnp.ones((n_ctx, HEAD_DIM), F32), cos], axis=0)
    sin = jnp.concatenate([jnp.zeros((n_ctx, HEAD_DIM), F32), sin], axis=0)
    return jnp.tile(cos, (1, 2)), jnp.tile(sin, (1, 2))


def _largest_divisor(n, cap, multiple):
    best = multiple
    for cand in range(multiple, cap + 1, multiple):
        if n % cand == 0:
            best = cand
    return best


def kernel(x, c, ctx, c_ctx, w_ada, b_ada, w_in, gla_gate_up_f, gla_gate_bias_f, gla_gate_up_b,
           gla_gate_bias_b, gla_norm_w, conv_w, att_sink, w_branch_a, w_branch_b, w_branch_c, w_out,
           ln1_w, ln1_b, ffn_up, ffn_conv, ffn_down, ln2_w, ln2_b):
    bsz, seq, d = x.shape
    n_ctx = ctx.shape[1]
    depth = w_in.shape[0]
    t = n_ctx + seq
    assert n_ctx % GLA_CHUNK == 0 and seq % GLA_CHUNK == 0 and seq >= 3 * ATT_BLOCK
    rb = _largest_divisor(t, 768, 128)
    rb_last = _largest_divisor(seq, 768, 128)
    bm = rb
    nsub = 2 if bm % 256 == 0 else 1
    nsub_merge = 3 if bm % 384 == 0 else nsub
    d_ff = ffn_down.shape[1]
    tn = 256
    nb_att = max(n for n in (1, 2, 3) if (t // ATT_BLOCK) % n == 0)

    cvec = jnp.zeros((16, d), F32).at[:bsz].set(c).at[bsz].set(c_ctx)
    mod = _modulation(cvec, w_ada, b_ada)
    mod_lat = mod[:, :bsz].reshape(depth, bsz, 6, d)
    mod_ctx = jnp.broadcast_to(mod[:, bsz].reshape(depth, 1, 6, d), (depth, bsz, 6, d))
    modb = jnp.concatenate([mod_ctx, mod_lat, jnp.zeros((depth, bsz, 4, d), F32)], axis=2)

    assert w_in.shape[2] == _OFF[-1]
    w_in_t = jnp.swapaxes(w_in, 1, 2).astype(ACT)
    zpad = jnp.zeros((depth, GATE_RANK, GLA_KW), F32)
    w_gate = jnp.concatenate(
        [jnp.concatenate([gla_gate_up_f, zpad], axis=2), jnp.concatenate([zpad, gla_gate_up_b], axis=2)],
        axis=1)
    b_gate = jnp.concatenate([gla_gate_bias_f, gla_gate_bias_b], axis=1).reshape(depth, 1, 2 * GLA_KW)

    cos, sin_signed = _rope_tables(n_ctx, seq)
    nw = gla_norm_w.reshape(depth, 1, GLA_DV)
    wa = w_branch_a.astype(ACT)
    wb = w_branch_b.astype(ACT)
    wc = w_branch_c.astype(ACT)
    wo = w_out.astype(ACT)
    w_up = ffn_up.astype(ACT)
    w_dn = ffn_down.astype(ACT)

    xs = None
    for l in range(depth):
        last = l == depth - 1
        if l == 0 and bm % n_ctx == 0:
            *proj, xs = _inproj(None, l, modb, w_in_t, w_gate, b_gate, n_ctx, bm, nsub, parts=(ctx, x))
        else:
            if xs is None:
                xs = jnp.concatenate([ctx, x], axis=1)
            proj = _inproj(xs, l, modb, w_in_t, w_gate, b_gate, n_ctx, bm, nsub)
        k, v, kva, q, r, conv3, qa, gates, g = proj
        ya = _gla(q, k, v, r, g, nw[l], n_ctx)
        yb = _short_conv(conv3, conv_w[l], n_ctx)
        yc = _attention(qa, kva, att_sink[l], cos, sin_signed, n_ctx, nb_att)
        xs = _merge(ya, yb, yc, gates, xs, l, modb, wa, wb, wc, wo,
                    ln1_w[l].reshape(1, d), ln1_b[l].reshape(1, d), n_ctx, bm, nsub_merge)
        xs = _ffn(xs, l, modb, w_up, ffn_conv[l], w_dn, ln2_w[l].reshape(1, d), ln2_b[l].reshape(1, d),
                  n_ctx, rb_last if last else rb, tn, row_off=n_ctx if last else 0)
    return xs
```

```python
import functools

import numpy as np
import jax
import jax.numpy as jnp
from jax import lax
from jax.experimental import pallas as pl
from jax.experimental.pallas import tpu as pltpu

DEPTH_FOR_DEEPNORM = 4
DEEPNORM_ALPHA = (2 * DEPTH_FOR_DEEPNORM) ** 0.25
LN_EPS = 1e-5
RMS_EPS = 1e-6
GATE_TAU = 16.0
NEG_INF = -1e30
ROPE_THETA = 10000.0
LOG2E = 1.4426950408889634

GRID_W = 64
GLA_HEADS = 4
GLA_DK = 64
GLA_DV = 128
GLA_KW = GLA_HEADS * GLA_DK
GLA_VW = GLA_HEADS * GLA_DV
GATE_RANK = 16
CONV_W = 512
ATT_HEADS = 8
ATT_KV_HEADS = 2
ATT_GROUP = ATT_HEADS // ATT_KV_HEADS
HEAD_DIM = 64
ATT_QW = ATT_HEADS * HEAD_DIM
ATT_KVW = ATT_KV_HEADS * HEAD_DIM
WINDOW = 128
ATT_BLOCK = 128

LANES = 128
GLA_CHUNK = 128
HALO = 16
VMEM_LIMIT = 56 * 1024 * 1024

ACT = jnp.bfloat16
F32 = jnp.float32


def _cparams(sem):
    return pltpu.CompilerParams(dimension_semantics=sem, vmem_limit_bytes=VMEM_LIMIT)


def _split_hi_lo(a):
    hi = a.astype(ACT)
    lo = (a - hi.astype(F32)).astype(ACT)
    return hi, lo


def _dot(a, b):
    return jnp.dot(a, b, preferred_element_type=F32)


def _dot_nt(a, b):
    return lax.dot_general(a, b, (((1,), (1,)), ((), ())), preferred_element_type=F32)


def _dot3(a, b):
    ah, al = _split_hi_lo(a)
    bh, bl = _split_hi_lo(b)
    return _dot(ah, bh) + _dot(ah, bl) + _dot(al, bh)


def _sigmoid(x):
    return 1.0 / (1.0 + jnp.exp(-x))


def _silu(x):
    return x * _sigmoid(x)


def _log_sigmoid(x):
    return jnp.minimum(x, 0.0) - jnp.log(1.0 + jnp.exp(-jnp.abs(x)))


def _layer_norm(y, w, b):
    mu = jnp.mean(y, axis=-1, keepdims=True)
    yc = y - mu
    var = jnp.mean(yc * yc, axis=-1, keepdims=True)
    return yc * lax.rsqrt(var + LN_EPS) * w + b


def _row_mod(mod, idx, row0, nrows, n_ctx):
    rows = row0 + lax.broadcasted_iota(jnp.int32, (nrows, 1), 0)
    return jnp.where(rows < n_ctx, mod[idx:idx + 1, :], mod[6 + idx:7 + idx, :])


def _mod_kernel(c_ref, w_ref, b_ref, o_ref):
    s = _silu(c_ref[...])
    o_ref[0] = _dot3(s, w_ref[0]) + b_ref[0]


def _modulation(cvec, w_ada, b_ada):
    depth, d, n = w_ada.shape
    tn = n // 4
    rows = cvec.shape[0]
    return pl.pallas_call(
        _mod_kernel,
        grid=(depth, n // tn),
        in_specs=[pl.BlockSpec((rows, d), lambda l, j: (0, 0)),
                  pl.BlockSpec((1, d, tn), lambda l, j: (l, 0, j)),
                  pl.BlockSpec((1, 1, tn), lambda l, j: (l, 0, j))],
        out_specs=pl.BlockSpec((1, rows, tn), lambda l, j: (l, 0, j)),
        out_shape=jax.ShapeDtypeStruct((depth, rows, n), F32),
        compiler_params=_cparams(("arbitrary", "arbitrary")),
    )(cvec, w_ada, b_ada.reshape(depth, 1, n))


_W_K, _W_V, _W_GLR, _W_KVA, _W_Q, _W_R, _W_CONV, _W_QA, _W_GATES = (
    GLA_KW, GLA_VW, 2 * GATE_RANK, 2 * ATT_KVW, GLA_KW, GLA_VW, 3 * CONV_W, ATT_QW, 3 * 1024)
_OFF = np.cumsum([0, _W_K, _W_V, _W_GLR, _W_KVA, _W_Q, _W_R, _W_CONV, _W_QA, _W_GATES]).tolist()


def _layer_spec(a, layer):
    tail = (0,) * (a.ndim - 1)
    return pl.BlockSpec((1,) + a.shape[1:], lambda *_: (layer,) + tail, pipeline_mode=pl.Buffered(1))


def _mod_spec(modb, layer):
    return pl.BlockSpec((1, 1) + modb.shape[2:], lambda b, *_: (layer, b, 0, 0))


def _inproj_kernel(n_ctx, bm, nsub, from_parts, *refs):
    if from_parts:
        npieces = bm // n_ctx
        ctx_ref, piece_refs, refs = refs[0], refs[1:1 + npieces], refs[1 + npieces:]
        x_ref = refs[-1]
        refs = refs[:-1]
        for s, p_ref in enumerate(piece_refs):
            rows_g = pl.program_id(1) * bm + s * n_ctx + lax.broadcasted_iota(jnp.int32, (n_ctx, 1), 0)
            x_ref[0, s * n_ctx:(s + 1) * n_ctx, :] = jnp.where(rows_g < n_ctx, ctx_ref[0], p_ref[0])
    else:
        x_ref, refs = refs[0], refs[1:]
    (mod_ref, wt_ref, wg_ref, bg_ref,
     k_ref, v_ref, kva_ref, q_ref, r_ref, conv_ref, qa_ref, gates_ref, g_ref) = refs
    mod = mod_ref[0, 0]
    sub_rows = bm // nsub
    for sub in range(nsub):
        rows = slice(sub * sub_rows, (sub + 1) * sub_rows)
        row0 = pl.program_id(1) * bm + sub * sub_rows
        sh = _row_mod(mod, 0, row0, sub_rows, n_ctx)
        sc = _row_mod(mod, 1, row0, sub_rows, n_ctx)
        h = (x_ref[0, rows, :] * (1.0 + sc) + sh).astype(ACT)

        def proj(c0, width):
            return _dot_nt(h, wt_ref[0, c0:c0 + width, :])

        z = _dot3(proj(_OFF[2], _W_GLR), wg_ref[0]) + bg_ref[0]
        g_ref[0, rows, :] = _log_sigmoid(z) * (1.0 / GATE_TAU)
        k_ref[0, rows, :] = proj(_OFF[0], _W_K).astype(ACT)
        v_ref[0, rows, :] = proj(_OFF[1], _W_V).astype(ACT)
        kva_ref[0, rows, :] = proj(_OFF[3], _W_KVA).astype(ACT)
        q_ref[0, rows, :] = (proj(_OFF[4], _W_Q) * (GLA_DK ** -0.5)).astype(ACT)
        r_ref[0, rows, :] = proj(_OFF[5], _W_R).astype(ACT)
        for i in range(_W_CONV // 512):
            conv_ref[0, rows, i * 512:(i + 1) * 512] = proj(_OFF[6] + i * 512, 512).astype(ACT)
        qa_ref[0, rows, :] = (proj(_OFF[7], _W_QA) * (HEAD_DIM ** -0.5)).astype(ACT)
        for i in range(_W_GATES // 1024):
            gates_ref[0, rows, i * 1024:(i + 1) * 1024] = proj(_OFF[8] + i * 1024, 1024).astype(ACT)


def _inproj(xs, layer, modb, wt, wg, bg, n_ctx, bm, nsub, parts=None):
    if parts is None:
        bsz, t, d = xs.shape
    else:
        ctx, x = parts
        bsz, seq, d = x.shape
        t = n_ctx + seq
        assert bm % n_ctx == 0
    row = lambda width: pl.BlockSpec((1, bm, width), lambda b, r: (b, r, 0))
    out = lambda width, dt: jax.ShapeDtypeStruct((bsz, t, width), dt)
    if parts is None:
        x_specs, x_args = [row(d)], [xs]
    else:
        npieces = bm // n_ctx
        piece = lambda s: pl.BlockSpec((1, n_ctx, d), lambda b, r: (b, jnp.maximum(r * npieces + s - 1, 0), 0))
        x_specs = [pl.BlockSpec((1, n_ctx, d), lambda b, r: (b, 0, 0))] + [piece(s) for s in range(npieces)]
        x_args = [ctx] + [x] * npieces
    out_specs = [row(_W_K), row(_W_V), row(_W_KVA), row(_W_Q), row(_W_R), row(_W_CONV), row(_W_QA),
                 row(_W_GATES), row(2 * GLA_KW)]
    out_shape = [out(_W_K, ACT), out(_W_V, ACT), out(_W_KVA, ACT), out(_W_Q, ACT), out(_W_R, ACT),
                 out(_W_CONV, ACT), out(_W_QA, ACT), out(_W_GATES, ACT), out(2 * GLA_KW, F32)]
    if parts is not None:
        out_specs.append(row(d))
        out_shape.append(out(d, F32))
    return pl.pallas_call(
        functools.partial(_inproj_kernel, n_ctx, bm, nsub, parts is not None),
        grid=(bsz, t // bm),
        in_specs=x_specs + [_mod_spec(modb, layer),
                            _layer_spec(wt, layer), _layer_spec(wg, layer), _layer_spec(bg, layer)],
        out_specs=out_specs,
        out_shape=out_shape,
        compiler_params=_cparams(("arbitrary", "arbitrary")),
    )(*x_args, modb, wt, wg, bg)


def _gla_kernel(n_ctx, q_ref, k_ref, v_ref, r_ref, g_ref, nw_ref, o_ref, b_scr, u_scr, dec_scr):
    t = q_ref.shape[1]
    L = GLA_CHUNK
    nch = t // L
    nch_ctx = n_ctx // L
    kw = GLA_KW

    ri = lax.broadcasted_iota(jnp.int32, (L, L), 0)
    ci = lax.broadcasted_iota(jnp.int32, (L, L), 1)
    tri_f = (ci <= ri).astype(ACT)
    tri_b = (ci >= ri).astype(ACT)
    lane_k = lax.broadcasted_iota(jnp.int32, (1, kw), 1)
    head_masks = [(lane_k >= h * GLA_DK) & (lane_k < (h + 1) * GLA_DK) for h in range(GLA_HEADS)]

    def rows(c):
        return pl.ds(pl.multiple_of(c * L, L), L)

    def pass1(c):
        rs = rows(c)
        g = g_ref[0, rs, :]
        g_hi, g_lo = _split_hi_lo(g)
        b_f = _dot(tri_f, g_hi[:, :kw]) + _dot(tri_f, g_lo[:, :kw])
        b_b = _dot(tri_b, g_hi[:, kw:]) + _dot(tri_b, g_lo[:, kw:])
        b_scr[rs, :kw] = b_f
        b_scr[rs, kw:] = b_b
        tot_f = b_f[L - 1:L, :]
        tot_b = b_b[0:1, :]
        yield
        k = k_ref[0, rs, :].astype(F32)
        ku_f = (k * jnp.exp(tot_f - b_f)).astype(ACT)
        ku_b = (k * jnp.exp(tot_b - b_b)).astype(ACT)
        ku = jnp.concatenate([ku_f, ku_b], axis=1)
        vt = v_ref[0, rs, :].astype(F32).T.astype(ACT)
        yield
        u_full = _dot(vt, ku)
        u_f = jnp.zeros((GLA_DV, kw), F32)
        u_b = jnp.zeros((GLA_DV, kw), F32)
        for h in range(GLA_HEADS):
            blk = u_full[h * GLA_DV:(h + 1) * GLA_DV, :]
            u_f = u_f + jnp.where(head_masks[h], blk[:, :kw], 0.0)
            u_b = u_b + jnp.where(head_masks[h], blk[:, kw:], 0.0)
        u_scr[c, 0] = u_f
        u_scr[c, 1] = u_b
        dec_scr[c, 0:1, :] = jnp.exp(tot_f)
        dec_scr[c, 1:2, :] = jnp.exp(tot_b)

    group = 6 if nch % 6 == 0 else 1

    def staged(chunk_stages):
        def body(gi, carry):
            chunks = [chunk_stages(gi * group + u) for u in range(group)]
            for _ in range(3):
                for stages in chunks:
                    next(stages, None)
            return carry
        lax.fori_loop(0, nch // group, body, 0)

    staged(pass1)

    def scan_f(c, s):
        u = u_scr[c, 0]
        u_scr[c, 0] = s
        return s * dec_scr[c, 0:1, :] + u

    lax.fori_loop(0, nch, scan_f, jnp.zeros((GLA_DV, kw), F32))

    def scan_b(i, s):
        c = jnp.where(i < nch_ctx, nch_ctx - 1 - i, nch - 1 - (i - nch_ctx))
        u = u_scr[c, 1]
        u_scr[c, 1] = s
        return s * dec_scr[c, 1:2, :] + u

    lax.fori_loop(0, nch, scan_b, jnp.zeros((GLA_DV, kw), F32))

    ri_st = lax.broadcasted_iota(jnp.int32, (GLA_HEADS * L, L), 0) & (L - 1)
    ci_st = lax.broadcasted_iota(jnp.int32, (GLA_HEADS * L, L), 1)
    tril = ci_st <= ri_st
    triu = ci_st >= ri_st
    nw = nw_ref[...]

    def pass2(c):
        rs = rows(c)
        q = q_ref[0, rs, :].astype(F32)
        k = k_ref[0, rs, :].astype(F32)
        v = v_ref[0, rs, :]
        b_f = b_scr[rs, :kw]
        b_b = b_scr[rs, kw:]
        e_f = jnp.exp(b_f)
        e_b = jnp.exp(b_b)
        qd_f = q * e_f
        qd_b = q * e_b
        ki_f = (k / e_f).astype(ACT)
        ki_b = (k / e_b).astype(ACT)
        qs_f = jnp.concatenate([jnp.where(m, qd_f, 0.0) for m in head_masks], axis=0).astype(ACT)
        qs_b = jnp.concatenate([jnp.where(m, qd_b, 0.0) for m in head_masks], axis=0).astype(ACT)
        yield
        p = (jnp.where(tril, _dot_nt(qs_f, ki_f), 0.0)
             + jnp.where(triu, _dot_nt(qs_b, ki_b), 0.0)).astype(ACT)
        yield
        s_f = u_scr[c, 0].astype(ACT)
        s_b = u_scr[c, 1].astype(ACT)
        inter = _dot_nt(qs_f, s_f) + _dot_nt(qs_b, s_b)
        outs = []
        for h in range(GLA_HEADS):
            o_h = inter[h * L:(h + 1) * L, :] + _dot(p[h * L:(h + 1) * L, :], v[:, h * GLA_DV:(h + 1) * GLA_DV])
            ms = jnp.mean(o_h * o_h, axis=-1, keepdims=True)
            outs.append(o_h * lax.rsqrt(ms + RMS_EPS) * nw)
        o = jnp.concatenate(outs, axis=1)
        o_ref[0, rs, :] = (o * _silu(r_ref[0, rs, :].astype(F32))).astype(o_ref.dtype)

    staged(pass2)


def _gla(q, k, v, r, g, nw, n_ctx):
    bsz, t, _ = q.shape
    nch = t // GLA_CHUNK
    full = lambda width: pl.BlockSpec((1, t, width), lambda b: (b, 0, 0))
    return pl.pallas_call(
        functools.partial(_gla_kernel, n_ctx),
        grid=(bsz,),
        in_specs=[full(GLA_KW), full(GLA_KW), full(GLA_VW), full(GLA_VW), full(2 * GLA_KW),
                  pl.BlockSpec((1, GLA_DV), lambda b: (0, 0))],
        out_specs=full(GLA_VW),
        out_shape=jax.ShapeDtypeStruct((bsz, t, GLA_VW), ACT),
        scratch_shapes=[pltpu.VMEM((t, 2 * GLA_KW), F32),
                        pltpu.VMEM((nch, 2, GLA_DV, GLA_KW), F32),
                        pltpu.VMEM((nch, 8, GLA_KW), F32)],
        compiler_params=_cparams(("arbitrary",)),
    )(q, k, v, r, g, nw)


def _conv_kernel(n_ctx, h_ref, gb_ref, gc_ref, w_ref, o_ref):
    t = h_ref.shape[1]
    a = gc_ref[0].astype(F32) * h_ref[0].astype(F32)
    rows = lax.broadcasted_iota(jnp.int32, (t, 1), 0)
    prev = jnp.where((rows == 0) | (rows == n_ctx), 0.0, pltpu.roll(a, 1, 0))
    nxt = jnp.where((rows == n_ctx - 1) | (rows == t - 1), 0.0, pltpu.roll(a, t - 1, 0))
    w = w_ref[...]
    y = w[0:1, :] * prev + w[1:2, :] * a + w[2:3, :] * nxt
    o_ref[0] = (gb_ref[0].astype(F32) * y).astype(o_ref.dtype)


def _short_conv(conv3, w, n_ctx):
    bsz, t, _ = conv3.shape
    spec = lambda i: pl.BlockSpec((1, t, CONV_W), lambda b: (b, 0, i))
    return pl.pallas_call(
        functools.partial(_conv_kernel, n_ctx),
        grid=(bsz,),
        in_specs=[spec(0), spec(1), spec(2), pl.BlockSpec(w.shape, lambda b: (0, 0))],
        out_specs=pl.BlockSpec((1, t, CONV_W), lambda b: (b, 0, 0)),
        out_shape=jax.ShapeDtypeStruct((bsz, t, CONV_W), ACT),
        compiler_params=_cparams(("arbitrary",)),
    )(conv3, conv3, conv3, w)


def _rope(x, cos, sin_signed):
    lane = lax.broadcasted_iota(jnp.int32, (1, LANES), 1)
    quarter = HEAD_DIM // 4
    swapped = jnp.where((lane % (2 * quarter)) < quarter,
                        pltpu.roll(x, LANES - quarter, 1), pltpu.roll(x, quarter, 1))
    return x * cos + swapped * sin_signed


def _attn_kernel(n_ctx, nb, sink_ref, q_ref, kv_ref, cosq_ref, sinq_ref, cos_all_ref, sin_all_ref, o_ref,
                 kd_scr, vd_scr, s_scr, p_scr):
    step = pl.program_id(1)

    @pl.when(step == 0)
    def _():
        lane = lax.broadcasted_iota(jnp.int32, (1, LANES), 1)
        first_half = lane < HEAD_DIM
        k = _rope(kv_ref[0, :, :LANES].astype(F32), cos_all_ref[...], sin_all_ref[...])
        v = kv_ref[0, :, LANES:].astype(F32)
        for src, dst in ((k, kd_scr), (v, vd_scr)):
            swapped = pltpu.roll(src, HEAD_DIM, 1)
            dst[0] = jnp.where(first_half, src, swapped).astype(ACT)
            dst[1] = jnp.where(first_half, swapped, src).astype(ACT)

    blocks = [_attn_block(n_ctx, step * nb + sub, sub, sink_ref, q_ref, cosq_ref, sinq_ref, o_ref, kd_scr, vd_scr,
                          s_scr.at[sub], p_scr.at[sub]) for sub in range(nb)]
    for _ in range(3):
        for blk_stages in blocks:
            next(blk_stages, None)


def _attn_block(n_ctx, j, sub, sink_ref, q_ref, cosq_ref, sinq_ref, o_ref, kd_scr, vd_scr, s_scr, p_scr):
    t = kd_scr.shape[1]
    blk = ATT_BLOCK
    n_loc = 3 * blk
    n_lat = t - n_ctx
    qrows = slice(sub * blk, (sub + 1) * blk)

    lane = lax.broadcasted_iota(jnp.int32, (1, LANES), 1)
    first_half = lane < HEAD_DIM
    cos = cosq_ref[qrows, :]
    sin = sinq_ref[qrows, :]
    qs = []
    for gi in range(ATT_HEADS // 2):
        qg = _rope(q_ref[0, qrows, gi * LANES:(gi + 1) * LANES].astype(F32), cos, sin)
        qs.append(jnp.where(first_half, qg, 0.0))
        qs.append(jnp.where(first_half, 0.0, qg))
    qst = jnp.concatenate(qs, axis=0).astype(ACT)

    jl = j - n_ctx // blk
    start = jnp.clip((jl - 1) * blk, 0, n_lat - n_loc)
    loc = pl.ds(pl.multiple_of(n_ctx + start, blk), n_loc)
    hrows = ATT_GROUP * blk
    for kv in range(ATT_KV_HEADS):
        rs = slice(kv * hrows, (kv + 1) * hrows)
        s_scr[rs, 0:n_ctx] = _dot_nt(qst[rs, :], kd_scr[kv, 0:n_ctx, :])
        s_scr[rs, n_ctx:] = _dot_nt(qst[rs, :], kd_scr[kv, loc, :])
    q_base = jnp.where(jl >= 0, jl * blk, -(t + WINDOW + blk))
    qpos = q_base + lax.broadcasted_iota(jnp.int32, (blk, n_loc), 0)
    kpos = start + lax.broadcasted_iota(jnp.int32, (blk, n_loc), 1)
    bias = jnp.where(jnp.abs(qpos - kpos) <= WINDOW, 0.0, NEG_INF)
    ones = jnp.ones((n_ctx + n_loc, LANES), ACT)
    yield

    sink_exp = []
    for i in range(ATT_HEADS):
        rows = slice(i * blk, (i + 1) * blk)
        sink = sink_ref[i] * LOG2E
        sc = s_scr[rows, 0:n_ctx]
        sl = s_scr[rows, n_ctx:] + bias
        m = jnp.maximum(jnp.maximum(jnp.max(sc, axis=-1, keepdims=True),
                                    jnp.max(sl, axis=-1, keepdims=True)), sink)
        p_scr[rows, 0:n_ctx] = jnp.exp2(sc - m).astype(ACT)
        p_scr[rows, n_ctx:] = jnp.exp2(sl - m).astype(ACT)
        sink_exp.append(jnp.exp2(sink - m))
    yield
    outs = []
    for kv in range(ATT_KV_HEADS):
        rs = slice(kv * hrows, (kv + 1) * hrows)
        v_aug = jnp.concatenate(
            [jnp.concatenate([vd_scr[kv, 0:n_ctx, :], vd_scr[kv, loc, :]], axis=0), ones], axis=1)
        o_kv = _dot(p_scr[rs, :], v_aug)
        for g in range(ATT_GROUP):
            i = kv * ATT_GROUP + g
            rows = slice(g * blk, (g + 1) * blk)
            outs.append(o_kv[rows, :LANES] / (o_kv[rows, LANES:] + sink_exp[i]))
    for gi in range(ATT_HEADS // 2):
        o_ref[0, qrows, gi * LANES:(gi + 1) * LANES] = jnp.where(
            first_half, outs[2 * gi], outs[2 * gi + 1]).astype(o_ref.dtype)


def _attention(qa, kva, sink, cos, sin_signed, n_ctx, nb):
    bsz, t, _ = qa.shape
    blk = ATT_BLOCK
    rows = nb * blk
    return pl.pallas_call(
        functools.partial(_attn_kernel, n_ctx, nb),
        grid=(bsz, t // rows),
        in_specs=[pl.BlockSpec(memory_space=pltpu.SMEM),
                  pl.BlockSpec((1, rows, ATT_QW), lambda b, j: (b, j, 0)),
                  pl.BlockSpec((1, t, 2 * ATT_KVW), lambda b, j: (b, 0, 0)),
                  pl.BlockSpec((rows, LANES), lambda b, j: (j, 0)),
                  pl.BlockSpec((rows, LANES), lambda b, j: (j, 0)),
                  pl.BlockSpec((t, LANES), lambda b, j: (0, 0)),
                  pl.BlockSpec((t, LANES), lambda b, j: (0, 0))],
        out_specs=pl.BlockSpec((1, rows, ATT_QW), lambda b, j: (b, j, 0)),
        out_shape=jax.ShapeDtypeStruct((bsz, t, ATT_QW), ACT),
        scratch_shapes=[pltpu.VMEM((ATT_KV_HEADS, t, LANES), ACT),
                        pltpu.VMEM((ATT_KV_HEADS, t, LANES), ACT),
                        pltpu.VMEM((nb, ATT_HEADS * blk, n_ctx + 3 * blk), F32),
                        pltpu.VMEM((nb, ATT_HEADS * blk, n_ctx + 3 * blk), ACT)],
        compiler_params=_cparams(("arbitrary", "arbitrary")),
    )(sink, qa, kva, cos * LOG2E, sin_signed * LOG2E, cos, sin_signed)


def _merge_kernel(n_ctx, bm, nsub, ya_ref, yb_ref, yc_ref, ma_ref, mb_ref, mc_ref, x_ref, mod_ref,
                  wa_ref, wb_ref, wc_ref, wo_ref, lnw_ref, lnb_ref, o_ref):
    sub_rows = bm // nsub
    parts = [slice(sub * sub_rows, (sub + 1) * sub_rows) for sub in range(nsub)]
    merged = []
    for rows in parts:
        m = (_sigmoid(ma_ref[0, rows, :].astype(F32)) * _dot(ya_ref[0, rows, :], wa_ref[0])
             + _sigmoid(mb_ref[0, rows, :].astype(F32)) * _dot(yb_ref[0, rows, :], wb_ref[0])
             + _sigmoid(mc_ref[0, rows, :].astype(F32)) * _dot(yc_ref[0, rows, :], wc_ref[0]))
        merged.append(m.astype(ACT))
    mixes = [_dot(m, wo_ref[0]) for m in merged]
    for sub, (rows, mix) in enumerate(zip(parts, mixes)):
        row0 = pl.program_id(1) * bm + sub * sub_rows
        gate = _row_mod(mod_ref[0, 0], 2, row0, sub_rows, n_ctx)
        o_ref[0, rows, :] = _layer_norm(DEEPNORM_ALPHA * x_ref[0, rows, :] + gate * mix,
                                        lnw_ref[...], lnb_ref[...])


def _merge(ya, yb, yc, gates, xs, layer, modb, wa, wb, wc, wo, lnw, lnb, n_ctx, bm, nsub):
    bsz, t, d = xs.shape
    row = lambda width, i=0: pl.BlockSpec((1, bm, width), lambda b, r: (b, r, i))
    const = lambda a: pl.BlockSpec(a.shape, lambda b, r: (0,) * a.ndim, pipeline_mode=pl.Buffered(1))
    return pl.pallas_call(
        functools.partial(_merge_kernel, n_ctx, bm, nsub),
        grid=(bsz, t // bm),
        in_specs=[row(GLA_VW), row(CONV_W), row(ATT_QW), row(d, 0), row(d, 1), row(d, 2), row(d),
                  _mod_spec(modb, layer),
                  _layer_spec(wa, layer), _layer_spec(wb, layer), _layer_spec(wc, layer),
                  _layer_spec(wo, layer), const(lnw), const(lnb)],
        out_specs=row(d),
        out_shape=jax.ShapeDtypeStruct((bsz, t, d), F32),
        compiler_params=_cparams(("arbitrary", "arbitrary")),
    )(ya, yb, yc, gates, gates, gates, xs, modb, wa, wb, wc, wo, lnw, lnb)


def _ffn_kernel(n_ctx, rb, tn, t, row_off, nx, *refs):
    x_refs = refs[:nx]
    (xp_ref, xn_ref, mod_ref, wup_ref, cw_ref, wd_ref, lnw_ref, lnb_ref, o_ref, h_scr, act_scr) = refs[nx:]
    piece = rb // nx
    d_ff = wd_ref.shape[1]
    r = pl.program_id(1)
    row0 = row_off + r * rb
    ext = rb + 2 * HALO
    edge = xp_ref.shape[1]
    mod = mod_ref[0, 0]

    def modulated(xv, first_row):
        n = xv.shape[0]
        return xv * (1.0 + _row_mod(mod, 4, first_row, n, n_ctx)) + _row_mod(mod, 3, first_row, n, n_ctx)

    def same_sequence(rows, neighbour):
        in_lat = (rows >= n_ctx).astype(jnp.int32)
        return (rows >= 0) & (rows < t) & (in_lat == (neighbour >= n_ctx).astype(jnp.int32))

    rows_p = row0 - edge + lax.broadcasted_iota(jnp.int32, (edge, 1), 0)
    rows_n = row0 + rb + lax.broadcasted_iota(jnp.int32, (edge, 1), 0)
    hp = jnp.where(same_sequence(rows_p, row0), modulated(xp_ref[0], row0 - edge), 0.0)
    hn = jnp.where(same_sequence(rows_n, row0 + rb - 1), modulated(xn_ref[0], row0 + rb), 0.0)
    zpad = jnp.zeros((HALO - edge, xp_ref.shape[2]), F32)
    h_scr[0:HALO, :] = jnp.concatenate([zpad, hp], axis=0).astype(ACT)
    for s, x_ref in enumerate(x_refs):
        h_scr[HALO + s * piece:HALO + (s + 1) * piece, :] = modulated(x_ref[0], row0 + s * piece).astype(ACT)
    h_scr[HALO + rb:ext, :] = jnp.concatenate([hn, zpad], axis=0).astype(ACT)

    def conv3(u, w, keep_prev=None, keep_next=None):
        prev = pltpu.roll(u, 1, 0)
        nxt = pltpu.roll(u, u.shape[0] - 1, 0)
        if keep_prev is not None:
            prev = jnp.where(keep_prev, prev, 0.0)
            nxt = jnp.where(keep_next, nxt, 0.0)
        return w[0:1, :] * prev + w[1:2, :] * u + w[2:3, :] * nxt

    lb = (n_ctx - row_off) % rb if n_ctx > row_off else 0
    fix = HALO + 8
    he = h_scr[...]
    nchunks = d_ff // tn

    def up(n):
        return (_dot(he, wup_ref[0, :, n * tn:(n + 1) * tn]),
                _dot(he, wup_ref[0, :, d_ff + n * tn:d_ff + (n + 1) * tn]))

    for n in range(nchunks):
        cg = slice(n * tn, (n + 1) * tn)
        cv = slice(d_ff + n * tn, d_ff + (n + 1) * tn)
        ug, uv = up(n)
        wg = cw_ref[:, cg]
        wv = cw_ref[:, cv]
        gate = conv3(ug, wg)[HALO:HALO + rb, :]
        val = conv3(uv, wv)[HALO:HALO + rb, :]
        act_scr[:, cg] = (_silu(gate) * val).astype(ACT)
        if lb:
            rows_g = row0 + lb - fix + lax.broadcasted_iota(jnp.int32, (2 * fix, 1), 0)
            win = slice(HALO + lb - fix, HALO + lb + fix)
            g_w = conv3(ug[win, :], wg, rows_g != n_ctx, rows_g != n_ctx - 1)
            v_w = conv3(uv[win, :], wv, rows_g != n_ctx, rows_g != n_ctx - 1)
            a_w = _silu(g_w) * v_w
            act_scr[lb - HALO:lb + HALO, cg] = a_w[fix - HALO:fix + HALO, :].astype(ACT)

    y = _dot(act_scr[...], wd_ref[0])
    for s, x_ref in enumerate(x_refs):
        rows = slice(s * piece, (s + 1) * piece)
        gate2 = _row_mod(mod, 5, row0 + s * piece, piece, n_ctx)
        o_ref[0, rows, :] = _layer_norm(DEEPNORM_ALPHA * x_ref[0] + gate2 * y[rows, :], lnw_ref[...], lnb_ref[...])


def _ffn(xs, layer, modb, w_up, w_conv, w_down, lnw, lnb, n_ctx, rb, tn, row_off=0):
    bsz, t, d = xs.shape
    d_ff = w_down.shape[1]
    edge = 8
    rows_out = t - row_off
    assert rows_out % rb == 0 and row_off % edge == 0
    lb = (n_ctx - row_off) % rb if n_ctx > row_off else 0
    assert lb == 0 or (2 * HALO <= lb <= rb - 2 * HALO and lb % HALO == 0)
    piece = int(np.gcd(rb, row_off)) if row_off else rb
    nx = rb // piece
    nb = rb // edge
    first = row_off // edge
    resident = lambda a: pl.BlockSpec(a.shape, lambda b, r: (0,) * a.ndim, pipeline_mode=pl.Buffered(1))
    x_piece = lambda s: pl.BlockSpec((1, piece, d), lambda b, r: (b, r * nx + s + row_off // piece, 0))
    return pl.pallas_call(
        functools.partial(_ffn_kernel, n_ctx, rb, tn, t, row_off, nx),
        grid=(bsz, rows_out // rb),
        in_specs=[x_piece(s) for s in range(nx)] + [
            pl.BlockSpec((1, edge, d), lambda b, r: (b, jnp.maximum(first + r * nb - 1, 0), 0)),
            pl.BlockSpec((1, edge, d), lambda b, r: (b, jnp.minimum(first + (r + 1) * nb, t // edge - 1), 0)),
            _mod_spec(modb, layer),
            _layer_spec(w_up, layer), resident(w_conv), _layer_spec(w_down, layer), resident(lnw), resident(lnb)],
        out_specs=pl.BlockSpec((1, rb, d), lambda b, r: (b, r, 0)),
        out_shape=jax.ShapeDtypeStruct((bsz, rows_out, d), F32),
        scratch_shapes=[pltpu.VMEM((rb + 2 * HALO, d), ACT), pltpu.VMEM((rb, d_ff), ACT)],
        compiler_params=_cparams(("arbitrary", "arbitrary")),
    )(*([xs] * (nx + 2)), modb, w_up, w_conv, w_down, lnw, lnb)


def _rope_tables(n_ctx, seq):
    rows = seq // GRID_W
    row_idx = np.repeat(np.arange(rows), GRID_W).astype(np.float32)
    col_idx = np.tile(np.arange(GRID_W), rows).astype(np.float32)
    half = HEAD_DIM // 2
    inv_freq = (ROPE_THETA ** (-np.arange(0, half, 2, dtype=np.float32) / half)).astype(np.float32)
    ang_r = jnp.asarray(row_idx[:, None] * inv_freq)
    ang_c = jnp.asarray(col_idx[:, None] * inv_freq)
    cos = jnp.concatenate([jnp.cos(ang_r)] * 2 + [jnp.cos(ang_c)] * 2, axis=-1)
    sin = jnp.concatenate([-jnp.sin(ang_r), jnp.sin(ang_r), -jnp.sin(ang_c), jnp.sin(ang_c)], axis=-1)
    cos = jnp.concatenate([jnp.ones((n_ctx, HEAD_DIM), F32), cos], axis=0)
    sin = jnp.concatenate([jnp.zeros((n_ctx, HEAD_DIM), F32), sin], axis=0)
    return jnp.tile(cos, (1, 2)), jnp.tile(sin, (1, 2))


def _largest_divisor(n, cap, multiple):
    best = multiple
    for cand in range(multiple, cap + 1, multiple):
        if n % cand == 0:
            best = cand
    return best


def kernel(x, c, ctx, c_ctx, w_ada, b_ada, w_in, gla_gate_up_f, gla_gate_bias_f, gla_gate_up_b,
           gla_gate_bias_b, gla_norm_w, conv_w, att_sink, w_branch_a, w_branch_b, w_branch_c, w_out,
           ln1_w, ln1_b, ffn_up, ffn_conv, ffn_down, ln2_w, ln2_b):
    bsz, seq, d = x.shape
    n_ctx = ctx.shape[1]
    depth = w_in.shape[0]
    t = n_ctx + seq
    assert n_ctx % GLA_CHUNK == 0 and seq % GLA_CHUNK == 0 and seq >= 3 * ATT_BLOCK
    rb = _largest_divisor(t, 768, 128)
    rb_last = _largest_divisor(seq, 768, 128)
    bm = rb
    nsub = 2 if bm % 256 == 0 else 1
    nsub_merge = 3 if bm % 384 == 0 else nsub
    d_ff = ffn_down.shape[1]
    tn = 256
    nb_att = max(n for n in (1, 2, 3, 6) if (t // ATT_BLOCK) % n == 0)

    cvec = jnp.zeros((16, d), F32).at[:bsz].set(c).at[bsz].set(c_ctx)
    mod = _modulation(cvec, w_ada, b_ada)
    mod_lat = mod[:, :bsz].reshape(depth, bsz, 6, d)
    mod_ctx = jnp.broadcast_to(mod[:, bsz].reshape(depth, 1, 6, d), (depth, bsz, 6, d))
    modb = jnp.concatenate([mod_ctx, mod_lat, jnp.zeros((depth, bsz, 4, d), F32)], axis=2)

    assert w_in.shape[2] == _OFF[-1]
    w_in_t = jnp.swapaxes(w_in, 1, 2).astype(ACT)
    zpad = jnp.zeros((depth, GATE_RANK, GLA_KW), F32)
    w_gate = jnp.concatenate(
        [jnp.concatenate([gla_gate_up_f, zpad], axis=2), jnp.concatenate([zpad, gla_gate_up_b], axis=2)],
        axis=1)
    b_gate = jnp.concatenate([gla_gate_bias_f, gla_gate_bias_b], axis=1).reshape(depth, 1, 2 * GLA_KW)

    cos, sin_signed = _rope_tables(n_ctx, seq)
    nw = gla_norm_w.reshape(depth, 1, GLA_DV)
    wa = w_branch_a.astype(ACT)
    wb = w_branch_b.astype(ACT)
    wc = w_branch_c.astype(ACT)
    wo = w_out.astype(ACT)
    w_up = ffn_up.astype(ACT)
    w_dn = ffn_down.astype(ACT)

    xs = None
    for l in range(depth):
        last = l == depth - 1
        if l == 0 and bm % n_ctx == 0:
            *proj, xs = _inproj(None, l, modb, w_in_t, w_gate, b_gate, n_ctx, bm, nsub, parts=(ctx, x))
        else:
            if xs is None:
                xs = jnp.concatenate([ctx, x], axis=1)
            proj = _inproj(xs, l, modb, w_in_t, w_gate, b_gate, n_ctx, bm, nsub)
        k, v, kva, q, r, conv3, qa, gates, g = proj
        ya = _gla(q, k, v, r, g, nw[l], n_ctx)
        yb = _short_conv(conv3, conv_w[l], n_ctx)
        yc = _attention(qa, kva, att_sink[l], cos, sin_signed, n_ctx, nb_att)
        xs = _merge(ya, yb, yc, gates, xs, l, modb, wa, wb, wc, wo,
                    ln1_w[l].reshape(1, d), ln1_b[l].reshape(1, d), n_ctx, bm, nsub_merge)
        xs = _ffn(xs, l, modb, w_up, ffn_conv[l], w_dn, ln2_w[l].reshape(1, d), ln2_b[l].reshape(1, d),
                  n_ctx, rb_last if last else rb, tn, row_off=n_ctx if last else 0)
    return xs
```
